```python
import math
import jax
import jax.numpy as jnp
from jax import lax
import numpy as np

D_MODEL = 2048
BATCH = 4
SEQ = 2048
DEPTH = 2

GRID_W = 64
CTX_LEN = 256
Q_BLOCK = 128
EPS = 1e-6
ROPE_BASE = 10000.0
N_MOD = 6

DA_HEADS = 6
DA_HALF_DIM = 64
DA_V_DIM = 2 * DA_HALF_DIM
DA_WIDTH = DA_HEADS * DA_V_DIM

POOL_WINDOWS = (2, 4, 8, 16)
POOL_GROUP_DIM = 128
POOL_WIDTH = len(POOL_WINDOWS) * POOL_GROUP_DIM

GQA_Q_HEADS = 6
GQA_KV_HEADS = 2
GQA_GROUP = GQA_Q_HEADS // GQA_KV_HEADS
GQA_HEAD_DIM = 128
GQA_WIDTH = GQA_Q_HEADS * GQA_HEAD_DIM

MIX_WIDTH = DA_WIDTH + POOL_WIDTH + GQA_WIDTH

IN_SIZES = (DA_HEADS * 2 * DA_HALF_DIM, DA_HEADS * 2 * DA_HALF_DIM, DA_HEADS * DA_V_DIM,
            POOL_WIDTH, GQA_WIDTH, GQA_KV_HEADS * GQA_HEAD_DIM, GQA_KV_HEADS * GQA_HEAD_DIM)
IN_COLS = sum(IN_SIZES)
IN_SPLITS = tuple(sum(IN_SIZES[:i + 1]) for i in range(len(IN_SIZES) - 1))

D_FF = 5632
CONV_W = 3

kernel_name = "hybrid_diffattn_pool_gqa_convffn_prefix_dit"


def rms_norm(x, gain):
    xf = x.astype(jnp.float32)
    y = xf * lax.rsqrt(jnp.mean(xf * xf, axis=-1, keepdims=True) + EPS)
    return (y * gain.astype(jnp.float32)).astype(x.dtype)


def modulate(x, gain, shift, scale):
    return rms_norm(x, gain) * (1.0 + scale) + shift


def axial_angles(row, col, rot_dim):
    axis_dim = rot_dim // 2
    freqs = ROPE_BASE ** (-(jnp.arange(axis_dim // 2, dtype=jnp.float32) * 2.0 / axis_dim))
    return row[:, None] * freqs, col[:, None] * freqs


def rope_1d(x, ang):
    x1, x2 = jnp.split(x, 2, axis=-1)
    cos = jnp.cos(ang).astype(x.dtype)
    sin = jnp.sin(ang).astype(x.dtype)
    return jnp.concatenate([x1 * cos - x2 * sin, x2 * cos + x1 * sin], axis=-1)


def rope_2d(x, ang_row, ang_col):
    xr, xc = jnp.split(x, 2, axis=-1)
    return jnp.concatenate([rope_1d(xr, ang_row), rope_1d(xc, ang_col)], axis=-1)


def rope_da(t, ang):
    t1, t2 = jnp.split(t, 2, axis=-1)
    return jnp.concatenate([rope_2d(t1, *ang), rope_2d(t2, *ang)], axis=-1)


def block_attention(q, k, v):
    b, hk, g, sq, dk = q.shape
    nb = sq // Q_BLOCK
    scale = dk ** -0.5
    qb = q.reshape(b, hk, g, nb, Q_BLOCK, dk).transpose(3, 0, 1, 2, 4, 5)

    def one_block(qblk):
        s = jnp.einsum('bhgqd,bhkd->bhgqk', qblk, k).astype(jnp.float32) * scale
        p = jax.nn.softmax(s, axis=-1).astype(v.dtype)
        return jnp.einsum('bhgqk,bhkd->bhgqd', p, v)

    out = lax.map(one_block, qb)
    return out.transpose(1, 2, 3, 0, 4, 5).reshape(b, hk, g, sq, v.shape[-1])


def diff_attention(q, k, v, lam, subln, lambda_init):
    d = DA_HALF_DIM
    a1 = block_attention(q[:, :, None, :, :d], k[..., :d], v)
    a2 = block_attention(q[:, :, None, :, d:], k[..., d:], v)
    o = (a1 - lam.astype(a1.dtype) * a2)[:, :, 0]
    o = rms_norm(o, subln) * (1.0 - lambda_init)
    b, h, sq, dv = o.shape
    return o.transpose(0, 2, 1, 3).reshape(b, sq, h * dv)


def gqa_attention(q, k, v):
    b, _, sq, d = q.shape
    o = block_attention(q.reshape(b, GQA_KV_HEADS, GQA_GROUP, sq, d), k, v)
    return o.reshape(b, GQA_Q_HEADS, sq, d).transpose(0, 2, 1, 3).reshape(b, sq, GQA_WIDTH)


def pool_mixer(u, w_pool, pool_scale):
    b, t, _ = u.shape
    uf = u.astype(jnp.float32)
    cs = jnp.pad(jnp.cumsum(uf, axis=1), ((0, 0), (1, 0), (0, 0)))
    pos = jnp.arange(t)
    outs = []
    for g, w in enumerate(POOL_WINDOWS):
        lo = jnp.clip(pos - w // 2, 0, t)
        hi = jnp.clip(pos - w // 2 + w, 0, t)
        csg = cs[..., g * POOL_GROUP_DIM:(g + 1) * POOL_GROUP_DIM]
        mean = (csg[:, hi] - csg[:, lo]) / (hi - lo).astype(jnp.float32)[None, :, None]
        outs.append(mean - uf[..., g * POOL_GROUP_DIM:(g + 1) * POOL_GROUP_DIM])
    pooled = jnp.stack(outs, axis=2).astype(u.dtype)
    mixed = jnp.einsum('btgc,gcd->btgd', pooled, w_pool).reshape(b, t, POOL_WIDTH)
    return mixed * pool_scale


def token_mixer(h_ctx, h_lat, ang_da, ang_gqa, lp, lambda_init, with_ctx_out):
    b, n_ctx, _ = h_ctx.shape
    n_tok = n_ctx + h_lat.shape[1]
    proj = jnp.concatenate([h_ctx, h_lat], axis=1) @ lp['w_in']
    da_q, da_k, da_v, pool_in, g_q, g_k, g_v = jnp.split(proj, IN_SPLITS, axis=-1)

    def heads(t, n, d):
        return t.reshape(b, n_tok, n, d).transpose(0, 2, 1, 3)

    q = heads(da_q, DA_HEADS, 2 * DA_HALF_DIM)
    k = heads(da_k, DA_HEADS, 2 * DA_HALF_DIM)
    v = heads(da_v, DA_HEADS, DA_V_DIM)
    q_lat = rope_da(q[:, :, n_ctx:], ang_da)
    k_all = jnp.concatenate([k[:, :, :n_ctx], rope_da(k[:, :, n_ctx:], ang_da)], axis=2)
    f32 = jnp.float32
    lam = (jnp.exp(jnp.sum(lp['lq1'].astype(f32) * lp['lk1'].astype(f32)))
           - jnp.exp(jnp.sum(lp['lq2'].astype(f32) * lp['lk2'].astype(f32))) + lambda_init)

    gq = rms_norm(heads(g_q, GQA_Q_HEADS, GQA_HEAD_DIM), lp['q_norm'])
    gk = rms_norm(heads(g_k, GQA_KV_HEADS, GQA_HEAD_DIM), lp['k_norm'])
    gv = heads(g_v, GQA_KV_HEADS, GQA_HEAD_DIM)
    gq_lat = rope_2d(gq[:, :, n_ctx:], *ang_gqa)
    gk_all = jnp.concatenate([gk[:, :, :n_ctx], rope_2d(gk[:, :, n_ctx:], *ang_gqa)], axis=2)

    def merge(o_da, o_pool, o_gqa):
        return jnp.concatenate([o_da, o_pool, rms_norm(o_gqa, lp['gqa_out_norm'])], axis=-1) @ lp['w_out']

    y_lat = merge(diff_attention(q_lat, k_all, v, lam, lp['subln'], lambda_init),
                  pool_mixer(pool_in[:, n_ctx:], lp['w_pool'], lp['pool_scale']),
                  gqa_attention(gq_lat, gk_all, gv))
    if not with_ctx_out:
        return None, y_lat
    y_ctx = merge(diff_attention(q[:, :, :n_ctx], k[:, :, :n_ctx], v[:, :, :n_ctx], lam, lp['subln'], lambda_init),
                  pool_mixer(pool_in[:, :n_ctx], lp['w_pool'], lp['pool_scale']),
                  gqa_attention(gq[:, :, :n_ctx], gk[:, :, :n_ctx], gv[:, :, :n_ctx]))
    return y_ctx, y_lat


def conv_ffn(h, w_up, conv_w, conv_b, w_down):
    t = h.shape[1]
    u = h @ w_up
    pad = CONV_W // 2
    up = jnp.pad(u, ((0, 0), (pad, CONV_W - 1 - pad), (0, 0)))
    u = sum(up[:, j:j + t] * conv_w[j] for j in range(CONV_W)) + conv_b
    gate, val = jnp.split(u, 2, axis=-1)
    return (jax.nn.silu(gate) * val) @ w_down


def setup_inputs(seed: int = 0) -> dict:
    key = jax.random.key(seed)
    ks = jax.random.split(key, 25)
    f32 = jnp.float32

    def nrm(k, shape, scale):
        return jax.random.normal(k, shape, f32) * scale

    def gain(k, shape):
        return 1.0 + 0.05 * jax.random.normal(k, shape, f32)

    return {
        'x': nrm(ks[0], (BATCH, SEQ, D_MODEL), 1.0),
        'c': nrm(ks[1], (BATCH, D_MODEL), 1.0),
        'ctx': nrm(ks[2], (BATCH, CTX_LEN, D_MODEL), 1.0),
        'c_ctx': nrm(ks[3], (D_MODEL,), 1.0),
        'w_mod': nrm(ks[4], (DEPTH, D_MODEL, N_MOD * D_MODEL), 0.5 * D_MODEL ** -0.5),
        'b_mod': nrm(ks[5], (DEPTH, N_MOD * D_MODEL), 0.01),
        'norm_mix': gain(ks[6], (DEPTH, D_MODEL)),
        'norm_ffn': gain(ks[7], (DEPTH, D_MODEL)),
        'w_in': nrm(ks[8], (DEPTH, D_MODEL, IN_COLS), D_MODEL ** -0.5),
        'da_lambda_q1': nrm(ks[9], (DEPTH, DA_HALF_DIM), 0.1),
        'da_lambda_k1': nrm(ks[10], (DEPTH, DA_HALF_DIM), 0.1),
        'da_lambda_q2': nrm(ks[11], (DEPTH, DA_HALF_DIM), 0.1),
        'da_lambda_k2': nrm(ks[12], (DEPTH, DA_HALF_DIM), 0.1),
        'da_subln': gain(ks[13], (DEPTH, DA_V_DIM)),
        'gqa_q_norm': gain(ks[14], (DEPTH, GQA_HEAD_DIM)),
        'gqa_k_norm': gain(ks[15], (DEPTH, GQA_HEAD_DIM)),
        'pool_w': nrm(ks[16], (DEPTH, len(POOL_WINDOWS), POOL_GROUP_DIM, POOL_GROUP_DIM), POOL_GROUP_DIM ** -0.5),
        'pool_scale': gain(ks[17], (DEPTH, POOL_WIDTH)),
        'gqa_out_norm': gain(ks[18], (DEPTH, GQA_WIDTH)),
        'w_out': nrm(ks[19], (DEPTH, MIX_WIDTH, D_MODEL), MIX_WIDTH ** -0.5),
        'w_up': nrm(ks[20], (DEPTH, D_MODEL, 2 * D_FF), D_MODEL ** -0.5),
        'conv_w': nrm(ks[21], (DEPTH, CONV_W, 2 * D_FF), CONV_W ** -0.5),
        'conv_b': nrm(ks[22], (DEPTH, 2 * D_FF), 0.01),
        'w_down': nrm(ks[23], (DEPTH, D_FF, D_MODEL), D_FF ** -0.5),
        'final_norm': gain(ks[24], (D_MODEL,)),
    }


def reference(x, c, ctx, c_ctx, w_mod, b_mod, norm_mix, norm_ffn, w_in,
              da_lambda_q1, da_lambda_k1, da_lambda_q2, da_lambda_k2, da_subln,
              gqa_q_norm, gqa_k_norm, pool_w, pool_scale, gqa_out_norm, w_out,
              w_up, conv_w, conv_b, w_down, final_norm):
    n_lat = x.shape[1]
    ROWS = n_lat // GRID_W
    row = jnp.repeat(jnp.arange(ROWS, dtype=jnp.float32), GRID_W)
    col = jnp.tile(jnp.arange(GRID_W, dtype=jnp.float32), ROWS)
    ang_da = axial_angles(row, col, DA_HALF_DIM)
    ang_gqa = axial_angles(row, col, GQA_HEAD_DIM)
    silu_c = jax.nn.silu(c)
    silu_c_ctx = jax.nn.silu(c_ctx)

    for l in range(DEPTH):
        last = l == DEPTH - 1
        lambda_init = 0.8 - 0.6 * math.exp(-0.3 * l)
        mod_lat = (silu_c @ w_mod[l] + b_mod[l])[:, None, :]
        mod_ctx = (silu_c_ctx @ w_mod[l] + b_mod[l])[None, None, :]
        sh1, sc1, g1, sh2, sc2, g2 = jnp.split(mod_lat, N_MOD, axis=-1)
        csh1, csc1, cg1, csh2, csc2, cg2 = jnp.split(mod_ctx, N_MOD, axis=-1)
        lp = dict(w_in=w_in[l], w_out=w_out[l], lq1=da_lambda_q1[l], lk1=da_lambda_k1[l],
                  lq2=da_lambda_q2[l], lk2=da_lambda_k2[l], subln=da_subln[l],
                  q_norm=gqa_q_norm[l], k_norm=gqa_k_norm[l], w_pool=pool_w[l],
                  pool_scale=pool_scale[l], gqa_out_norm=gqa_out_norm[l])

        h_lat = modulate(x, norm_mix[l], sh1, sc1)
        h_ctx = modulate(ctx, norm_mix[l], csh1, csc1)
        y_ctx, y_lat = token_mixer(h_ctx, h_lat, ang_da, ang_gqa, lp, lambda_init, not last)
        x = x + g1 * y_lat
        x = x + g2 * conv_ffn(modulate(x, norm_ffn[l], sh2, sc2), w_up[l], conv_w[l], conv_b[l], w_down[l])
        if not last:
            ctx = ctx + cg1 * y_ctx
            ctx = ctx + cg2 * conv_ffn(modulate(ctx, norm_ffn[l], csh2, csc2), w_up[l], conv_w[l], conv_b[l], w_down[l])

    return rms_norm(x, final_norm)
```

```python
import functools
import math

import jax
import jax.numpy as jnp
from jax import lax
from jax.experimental import pallas as pl
from jax.experimental.pallas import tpu as pltpu

F32 = jnp.float32
BF16 = jnp.bfloat16

D_MODEL = 2048
GRID_W = 64
EPS = 1e-6
ROPE_BASE = 10000.0
N_MOD = 6

DA_HEADS = 6
DA_HALF_DIM = 64
HEAD_DIM = 128
DA_WIDTH = DA_HEADS * HEAD_DIM
POOL_WINDOWS = (2, 4, 8, 16)
POOL_WIDTH = len(POOL_WINDOWS) * HEAD_DIM
GQA_Q_HEADS = 6
GQA_KV_HEADS = 2
GQA_GROUP = GQA_Q_HEADS // GQA_KV_HEADS
GQA_WIDTH = GQA_Q_HEADS * HEAD_DIM
GQA_KV_WIDTH = GQA_KV_HEADS * HEAD_DIM
IN_COLS = 3 * DA_WIDTH + POOL_WIDTH + GQA_WIDTH + 2 * GQA_KV_WIDTH
D_FF = 5632
CONV_W = 3

OFF_DAQ = 0
OFF_DAK = OFF_DAQ + DA_WIDTH
OFF_DAV = OFF_DAK + DA_WIDTH
OFF_POOL = OFF_DAV + DA_WIDTH
OFF_GQ = OFF_POOL + POOL_WIDTH
OFF_GK = OFF_GQ + GQA_WIDTH
OFF_GV = OFF_GK + GQA_KV_WIDTH

V7X_VMEM_LIMIT_BYTES = 56 * 1024 * 1024
HALO = 16
FFN_TF = 512


def _cparams(*sem):
    return pltpu.CompilerParams(dimension_semantics=sem, vmem_limit_bytes=V7X_VMEM_LIMIT_BYTES)


def _dot(a, b):
    return jnp.dot(a, b, preferred_element_type=F32)


def _dot_nt(a, b):
    return lax.dot_general(a, b, (((1,), (1,)), ((), ())), preferred_element_type=F32)


def _rms(x, gain):
    return x * lax.rsqrt(jnp.mean(x * x, axis=-1, keepdims=True) + EPS) * gain


def _modulate(x, gain, shift, scale):
    return _rms(x, gain) * (1.0 + scale) + shift


def _silu(x):
    return x / (1.0 + jnp.exp(-x))


def _mod_kernel(c_ref, w_ref, b_ref, o_ref):
    s = _silu(c_ref[...]).astype(BF16)
    o_ref[0] = _dot(s, w_ref[0].astype(BF16)) + b_ref[0]


def _mod_call(c8, w_mod, b_mod, tn=1024):
    depth, d, n = w_mod.shape
    return pl.pallas_call(
        _mod_kernel,
        grid=(depth, n // tn),
        in_specs=[pl.BlockSpec((8, d), lambda l, j: (0, 0)),
                  pl.BlockSpec((1, d, tn), lambda l, j: (l, 0, j)),
                  pl.BlockSpec((1, 1, tn), lambda l, j: (l, 0, j))],
        out_specs=pl.BlockSpec((1, 8, tn), lambda l, j: (l, 0, j)),
        out_shape=jax.ShapeDtypeStruct((depth, 8, n), F32),
        compiler_params=_cparams("parallel", "parallel"),
        name="adaln_mod",
    )(c8, w_mod, b_mod.reshape(depth, 1, n))


def _rope(x, c, s1, s2, shift):
    return x * c + pltpu.roll(x, HEAD_DIM - shift, 1) * s1 + pltpu.roll(x, shift, 1) * s2


def _inproj_kernel(*refs, use_rope):
    x_ref, sh_ref, sc_ref, nrm_ref, w_ref, qn_ref, kn_ref = refs[:7]
    pos = 7
    if use_rope:
        dac, das1, das2, gc, gs1, gs2 = (r[...] for r in refs[pos:pos + 6])
        pos += 6
    daq_ref, dak_ref, dav_ref, pool_ref, gq_ref, gk_ref, gv_ref = refs[pos:]

    h = _modulate(x_ref[...], nrm_ref[...], sh_ref[0], sc_ref[0]).astype(BF16)

    def proj(off, width):
        return _dot(h, w_ref[:, off:off + width])

    def head(a, i):
        return a[:, i * HEAD_DIM:(i + 1) * HEAD_DIM]

    a = proj(OFF_DAQ, DA_WIDTH)
    for i in range(DA_HEADS):
        t = head(a, i)
        if use_rope:
            t = _rope(t, dac, das1, das2, DA_HALF_DIM // 4)
        daq_ref[:, i * HEAD_DIM:(i + 1) * HEAD_DIM] = (t * (DA_HALF_DIM ** -0.5)).astype(BF16)
    a = proj(OFF_DAK, DA_WIDTH)
    for i in range(DA_HEADS):
        t = head(a, i)
        if use_rope:
            t = _rope(t, dac, das1, das2, DA_HALF_DIM // 4)
        dak_ref[:, i * HEAD_DIM:(i + 1) * HEAD_DIM] = t.astype(BF16)
    dav_ref[...] = proj(OFF_DAV, DA_WIDTH).astype(BF16)
    pool_ref[...] = proj(OFF_POOL, POOL_WIDTH)
    a = proj(OFF_GQ, GQA_WIDTH)
    for i in range(GQA_Q_HEADS):
        t = _rms(head(a, i), qn_ref[...])
        if use_rope:
            t = _rope(t, gc, gs1, gs2, HEAD_DIM // 4)
        gq_ref[:, i * HEAD_DIM:(i + 1) * HEAD_DIM] = (t * (HEAD_DIM ** -0.5)).astype(BF16)
    a = proj(OFF_GK, GQA_KV_WIDTH)
    for i in range(GQA_KV_HEADS):
        t = _rms(head(a, i), kn_ref[...])
        if use_rope:
            t = _rope(t, gc, gs1, gs2, HEAD_DIM // 4)
        gk_ref[:, i * HEAD_DIM:(i + 1) * HEAD_DIM] = t.astype(BF16)
    gv_ref[...] = proj(OFF_GV, GQA_KV_WIDTH).astype(BF16)


def _inproj_call(x, shift, scale, norm_gain, w_in_bf, q_norm, k_norm, rope, tm, rows_per_mod):
    m, d = x.shape
    tpm = rows_per_mod // tm
    row = lambda i: (i, 0)
    const = lambda i: (0, 0)
    in_specs = [pl.BlockSpec((tm, d), row),
                pl.BlockSpec((1, 1, d), lambda i: (i // tpm, 0, 0)),
                pl.BlockSpec((1, 1, d), lambda i: (i // tpm, 0, 0)),
                pl.BlockSpec((1, d), const),
                pl.BlockSpec((d, IN_COLS), const),
                pl.BlockSpec((1, HEAD_DIM), const),
                pl.BlockSpec((1, HEAD_DIM), const)]
    args = [x, shift, scale, norm_gain.reshape(1, d), w_in_bf,
            q_norm.reshape(1, HEAD_DIM), k_norm.reshape(1, HEAD_DIM)]
    if rope is not None:
        tps = rope[0].shape[0] // tm
        in_specs += [pl.BlockSpec((tm, HEAD_DIM), lambda i: (i % tps, 0))] * 6
        args += list(rope)
    widths = (DA_WIDTH, DA_WIDTH, DA_WIDTH, POOL_WIDTH, GQA_WIDTH, GQA_KV_WIDTH, GQA_KV_WIDTH)
    dtypes = (BF16, BF16, BF16, F32, BF16, BF16, BF16)
    return pl.pallas_call(
        functools.partial(_inproj_kernel, use_rope=rope is not None),
        grid=(m // tm,),
        in_specs=in_specs,
        out_specs=[pl.BlockSpec((tm, w), row) for w in widths],
        out_shape=[jax.ShapeDtypeStruct((m, w), dt) for w, dt in zip(widths, dtypes)],
        compiler_params=_cparams("parallel"),
        name="in_proj",
    )(*args)


def _softmax_pv(q, ks, vs):
    ss = [_dot_nt(q, k) for k in ks]
    m = functools.reduce(jnp.maximum, [jnp.max(s, axis=-1, keepdims=True) for s in ss])
    es = [jnp.exp(s - m) for s in ss]
    l = functools.reduce(jnp.add, [jnp.sum(e, axis=-1, keepdims=True) for e in es])
    o = functools.reduce(jnp.add, [_dot(e.astype(BF16), v) for e, v in zip(es, vs)])
    return o / l


def _da_attn_kernel(*refs, nseg, lambda_init):
    q_ref = refs[0]
    k_refs = refs[1:1 + nseg]
    v_refs = refs[1 + nseg:1 + 2 * nseg]
    lq1, lk1, lq2, lk2, subln_ref, o_ref = refs[1 + 2 * nseg:]
    q = q_ref[...]
    ks = [r[...] for r in k_refs]
    vs = [r[...] for r in v_refs]
    lane = lax.broadcasted_iota(jnp.int32, q.shape, 1)
    zero = jnp.zeros_like(q)
    a1 = _softmax_pv(jnp.where(lane < DA_HALF_DIM, q, zero), ks, vs)
    a2 = _softmax_pv(jnp.where(lane >= DA_HALF_DIM, q, zero), ks, vs)
    lam = (jnp.exp(jnp.sum(lq1[...] * lk1[...], axis=-1, keepdims=True))
           - jnp.exp(jnp.sum(lq2[...] * lk2[...], axis=-1, keepdims=True)) + lambda_init)
    o = _rms(a1 - lam * a2, subln_ref[...]) * (1.0 - lambda_init)
    o_ref[...] = o.astype(BF16)


def _da_attn_call(q, k_segs, v_segs, seg_lens, lam_vecs, subln, lambda_init, sq, tq):
    mq = q.shape[0]
    nb = mq // sq
    nq = sq // tq
    nseg = len(k_segs)
    qspec = pl.BlockSpec((tq, HEAD_DIM), lambda b, h, i: (b * nq + i, h))
    kv_specs = [pl.BlockSpec((n, HEAD_DIM), lambda b, h, i: (b, h)) for n in seg_lens]
    vec = lambda n: pl.BlockSpec((1, n), lambda b, h, i: (0, 0))
    return pl.pallas_call(
        functools.partial(_da_attn_kernel, nseg=nseg, lambda_init=lambda_init),
        grid=(nb, DA_HEADS, nq),
        in_specs=[qspec] + kv_specs + kv_specs + [vec(DA_HALF_DIM)] * 4 + [vec(HEAD_DIM)],
        out_specs=qspec,
        out_shape=jax.ShapeDtypeStruct((mq, DA_WIDTH), BF16),
        compiler_params=_cparams("parallel", "parallel", "parallel"),
        name="da_attn",
    )(q, *k_segs, *v_segs, *[v.reshape(1, DA_HALF_DIM) for v in lam_vecs], subln.reshape(1, HEAD_DIM))


def _gqa_attn_kernel(*refs, nseg):
    q_ref = refs[0]
    ks = [r[...] for r in refs[1:1 + nseg]]
    vs = [r[...] for r in refs[1 + nseg:1 + 2 * nseg]]
    o_ref = refs[1 + 2 * nseg]
    for g in range(GQA_GROUP):
        sl = slice(g * HEAD_DIM, (g + 1) * HEAD_DIM)
        o_ref[:, sl] = _softmax_pv(q_ref[:, sl], ks, vs)


def _gqa_attn_call(q, k_segs, v_segs, seg_lens, sq, tq):
    mq = q.shape[0]
    nb = mq // sq
    nq = sq // tq
    nseg = len(k_segs)
    qspec = pl.BlockSpec((tq, GQA_GROUP * HEAD_DIM), lambda b, h, i: (b * nq + i, h))
    kv_specs = [pl.BlockSpec((n, HEAD_DIM), lambda b, h, i: (b, h)) for n in seg_lens]
    return pl.pallas_call(
        functools.partial(_gqa_attn_kernel, nseg=nseg),
        grid=(nb, GQA_KV_HEADS, nq),
        in_specs=[qspec] + kv_specs + kv_specs,
        out_specs=qspec,
        out_shape=jax.ShapeDtypeStruct((mq, GQA_WIDTH), F32),
        compiler_params=_cparams("parallel", "parallel", "parallel"),
        name="gqa_attn",
    )(q, *k_segs, *v_segs)


def _pool_kernel(u_ref, w_ref, ps_ref, o_ref):
    t = u_ref.shape[0]
    row = lax.broadcasted_iota(jnp.int32, (t, HEAD_DIM), 0)
    for g, w in enumerate(POOL_WINDOWS):
        sl = slice(g * HEAD_DIM, (g + 1) * HEAD_DIM)
        u = u_ref[:, sl]
        acc = jnp.zeros_like(u)
        for d in range(-(w // 2), w - w // 2):
            if d == 0:
                acc = acc + u
                continue
            shifted = pltpu.roll(u, (-d) % t, 0)
            valid = (row >= -d) if d < 0 else (row < t - d)
            acc = acc + jnp.where(valid, shifted, 0.0)
        cnt = jnp.minimum(row - w // 2 + w, t) - jnp.maximum(row - w // 2, 0)
        pooled = acc / cnt.astype(F32) - u
        mixed = _dot(pooled.astype(BF16), w_ref[g].astype(BF16))
        o_ref[:, sl] = (mixed * ps_ref[:, sl]).astype(BF16)


def _pool_call(u, w_pool, pool_scale, seq):
    m = u.shape[0]
    return pl.pallas_call(
        _pool_kernel,
        grid=(m // seq,),
        in_specs=[pl.BlockSpec((seq, POOL_WIDTH), lambda b: (b, 0)),
                  pl.BlockSpec(w_pool.shape, lambda b: (0, 0, 0)),
                  pl.BlockSpec((1, POOL_WIDTH), lambda b: (0, 0))],
        out_specs=pl.BlockSpec((seq, POOL_WIDTH), lambda b: (b, 0)),
        out_shape=jax.ShapeDtypeStruct((m, POOL_WIDTH), BF16),
        compiler_params=_cparams("parallel"),
        name="pool_mixer",
    )(u, w_pool, pool_scale.reshape(1, POOL_WIDTH))


def _outproj_kernel(oda_ref, opool_ref, ogqa_ref, gn_ref, w_ref, x_ref, g1_ref,
                    nrm2_ref, sh2_ref, sc2_ref, xo_ref, h2_ref):
    ogn = _rms(ogqa_ref[...], gn_ref[...]).astype(BF16)
    cat = jnp.concatenate([oda_ref[...], opool_ref[...], ogn], axis=1)
    xn = x_ref[...] + g1_ref[0] * _dot(cat, w_ref[...])
    xo_ref[...] = xn
    h2_ref[...] = _modulate(xn, nrm2_ref[...], sh2_ref[0], sc2_ref[0]).astype(BF16)


def _outproj_call(o_da, o_pool, o_gqa, gqa_out_norm, w_out_bf, x, g1, norm_ffn, sh2, sc2,
                  tm, rows_per_mod):
    m, d = x.shape
    tpm = rows_per_mod // tm
    row = lambda i: (i, 0)
    const = lambda i: (0, 0)
    modrow = pl.BlockSpec((1, 1, d), lambda i: (i // tpm, 0, 0))
    return pl.pallas_call(
        _outproj_kernel,
        grid=(m // tm,),
        in_specs=[pl.BlockSpec((tm, DA_WIDTH), row),
                  pl.BlockSpec((tm, POOL_WIDTH), row),
                  pl.BlockSpec((tm, GQA_WIDTH), row),
                  pl.BlockSpec((1, GQA_WIDTH), const),
                  pl.BlockSpec((d, d), const),
                  pl.BlockSpec((tm, d), row),
                  modrow,
                  pl.BlockSpec((1, d), const),
                  modrow, modrow],
        out_specs=[pl.BlockSpec((tm, d), row), pl.BlockSpec((tm, d), row)],
        out_shape=[jax.ShapeDtypeStruct((m, d), F32), jax.ShapeDtypeStruct((m, d), BF16)],
        compiler_params=_cparams("parallel"),
        name="out_proj",
    )(o_da, o_pool, o_gqa, gqa_out_norm.reshape(1, GQA_WIDTH), w_out_bf, x, g1,
      norm_ffn.reshape(1, d), sh2, sc2)


def _ffn_kernel(*refs, tm, tf, tiles_per_seq, final):
    h_ref, hp_ref, hn_ref, wu_ref, cw_ref, cb_ref, wd_ref, x_ref, g2_ref = refs[:9]
    pos = 9
    if final:
        fn_ref = refs[pos]
        pos += 1
    o_ref, lhs_ref = refs[pos:]
    i = pl.program_id(0)
    j = pl.program_id(1)

    @pl.when(j == 0)
    def _():
        first = (i % tiles_per_seq) == 0
        last = (i % tiles_per_seq) == tiles_per_seq - 1
        lhs_ref[0:HALO, :] = jnp.where(first, jnp.zeros_like(hp_ref[...]), hp_ref[...])
        lhs_ref[HALO:HALO + tm, :] = h_ref[...]
        lhs_ref[HALO + tm:, :] = jnp.where(last, jnp.zeros_like(hn_ref[...]), hn_ref[...])
        o_ref[...] = jnp.zeros_like(o_ref)

    u = _dot(lhs_ref[...], wu_ref[...])
    cw = cw_ref[...]
    conv = (u[HALO - 1:HALO - 1 + tm] * cw[0:1] + u[HALO:HALO + tm] * cw[1:2]
            + u[HALO + 1:HALO + 1 + tm] * cw[2:3] + cb_ref[...])
    act = (_silu(conv[:, :tf]) * conv[:, tf:]).astype(BF16)
    o_ref[...] += _dot(act, wd_ref[...])

    @pl.when(j == pl.num_programs(1) - 1)
    def _():
        out = x_ref[...] + g2_ref[0] * o_ref[...]
        if final:
            out = _rms(out, fn_ref[...])
        o_ref[...] = out


def _ffn_call(h2, w_up_t, conv_w_t, conv_b_t, w_down_bf, x, g2, final_norm, tm, seq, rows_per_mod):
    m, d = x.shape
    tf = FFN_TF
    nj = D_FF // tf
    tps = seq // tm
    tpm = rows_per_mod // tm
    hb = tm // HALO
    nhb = m // HALO
    final = final_norm is not None
    row = lambda i, j: (i, 0)
    in_specs = [pl.BlockSpec((tm, d), row),
                pl.BlockSpec((HALO, d), lambda i, j: (jnp.maximum(i * hb - 1, 0), 0)),
                pl.BlockSpec((HALO, d), lambda i, j: (jnp.minimum((i + 1) * hb, nhb - 1), 0)),
                pl.BlockSpec((d, 2 * tf), lambda i, j: (0, j)),
                pl.BlockSpec((CONV_W, 2 * tf), lambda i, j: (0, j)),
                pl.BlockSpec((1, 2 * tf), lambda i, j: (0, j)),
                pl.BlockSpec((tf, d), lambda i, j: (j, 0)),
                pl.BlockSpec((tm, d), row),
                pl.BlockSpec((1, 1, d), lambda i, j: (i // tpm, 0, 0))]
    args = [h2, h2, h2, w_up_t, conv_w_t, conv_b_t, w_down_bf, x, g2]
    if final:
        in_specs.append(pl.BlockSpec((1, d), lambda i, j: (0, 0)))
        args.append(final_norm.reshape(1, d))
    return pl.pallas_call(
        functools.partial(_ffn_kernel, tm=tm, tf=tf, tiles_per_seq=tps, final=final),
        grid=(m // tm, nj),
        in_specs=in_specs,
        out_specs=pl.BlockSpec((tm, d), row),
        out_shape=jax.ShapeDtypeStruct((m, d), F32),
        scratch_shapes=[pltpu.VMEM((tm + 2 * HALO, d), BF16)],
        compiler_params=_cparams("parallel", "arbitrary"),
        name="conv_ffn",
    )(*args)


def _rope_tables(seq, rot_dim):
    rows = seq // GRID_W
    row = jnp.repeat(jnp.arange(rows, dtype=F32), GRID_W)
    col = jnp.tile(jnp.arange(GRID_W, dtype=F32), rows)
    axis_dim = rot_dim // 2
    nf = axis_dim // 2
    freqs = ROPE_BASE ** (-(jnp.arange(nf, dtype=F32) * 2.0 / axis_dim))
    ar, ac = row[:, None] * freqs, col[:, None] * freqs
    cos = jnp.concatenate([jnp.cos(ar)] * 2 + [jnp.cos(ac)] * 2, axis=-1)
    sin = jnp.concatenate([jnp.sin(ar)] * 2 + [jnp.sin(ac)] * 2, axis=-1)
    first = jnp.tile(jnp.concatenate([jnp.ones((nf,), F32), jnp.zeros((nf,), F32)]), 2)
    reps = HEAD_DIM // rot_dim
    return (jnp.tile(cos, (1, reps)), jnp.tile(-sin * first, (1, reps)),
            jnp.tile(sin * (1.0 - first), (1, reps)))


def _tile_ffn_cols(a):
    lead = a.shape[:-1]
    nj = D_FF // FFN_TF
    a = a.reshape(lead + (2, nj, FFN_TF))
    return jnp.swapaxes(a, -3, -2).reshape(lead + (2 * D_FF,))


def kernel(x, c, ctx, c_ctx, w_mod, b_mod, norm_mix, norm_ffn, w_in, da_lambda_q1, da_lambda_k1,
           da_lambda_q2, da_lambda_k2, da_subln, gqa_q_norm, gqa_k_norm, pool_w, pool_scale,
           gqa_out_norm, w_out, w_up, conv_w, conv_b, w_down, final_norm):
    nb, seq, d = x.shape
    n_ctx = ctx.shape[1]
    depth = w_mod.shape[0]
    assert d == D_MODEL and seq % GRID_W == 0 and nb <= 7

    tm_lat = min(512, seq)
    tm_ctx = min(256, n_ctx)
    tq_lat = min(256, seq)
    tq_ctx = min(256, n_ctx)

    c8 = jnp.zeros((8, d), F32).at[:nb].set(c).at[nb].set(c_ctx)
    mods = _mod_call(c8, w_mod, b_mod)

    rope_da = _rope_tables(seq, DA_HALF_DIM)
    rope_gqa = _rope_tables(seq, HEAD_DIM)
    rope = rope_da + rope_gqa

    xl = x.reshape(nb * seq, d)
    xc = ctx.reshape(nb * n_ctx, d)

    for l in range(depth):
        last = l == depth - 1
        lambda_init = 0.8 - 0.6 * math.exp(-0.3 * l)
        m = mods[l]
        lat = [m[:nb, k * d:(k + 1) * d].reshape(nb, 1, d) for k in range(N_MOD)]
        cm = [m[nb:nb + 1, k * d:(k + 1) * d].reshape(1, 1, d) for k in range(N_MOD)]
        w_in_bf = w_in[l].astype(BF16)
        w_out_bf = w_out[l].astype(BF16)
        w_up_t = _tile_ffn_cols(w_up[l].astype(BF16))
        conv_w_t = _tile_ffn_cols(conv_w[l])
        conv_b_t = _tile_ffn_cols(conv_b[l]).reshape(1, 2 * D_FF)
        w_down_bf = w_down[l].astype(BF16)
        lam_vecs = (da_lambda_q1[l], da_lambda_k1[l], da_lambda_q2[l], da_lambda_k2[l])

        pl_ = _inproj_call(xl, lat[0], lat[1], norm_mix[l], w_in_bf, gqa_q_norm[l], gqa_k_norm[l],
                           rope, tm_lat, seq)
        pc_ = _inproj_call(xc, cm[0], cm[1], norm_mix[l], w_in_bf, gqa_q_norm[l], gqa_k_norm[l],
                           None, tm_ctx, nb * n_ctx)
        daq_l, dak_l, dav_l, pool_l, gq_l, gk_l, gv_l = pl_
        daq_c, dak_c, dav_c, pool_c, gq_c, gk_c, gv_c = pc_

        o_da = _da_attn_call(daq_l, [dak_c, dak_l], [dav_c, dav_l], [n_ctx, seq], lam_vecs,
                             da_subln[l], lambda_init, seq, tq_lat)
        o_gqa = _gqa_attn_call(gq_l, [gk_c, gk_l], [gv_c, gv_l], [n_ctx, seq], seq, tq_lat)
        o_pool = _pool_call(pool_l, pool_w[l], pool_scale[l], seq)
        xl, h2 = _outproj_call(o_da, o_pool, o_gqa, gqa_out_norm[l], w_out_bf, xl, lat[2],
                               norm_ffn[l], lat[3], lat[4], tm_lat, seq)
        xl = _ffn_call(h2, w_up_t, conv_w_t, conv_b_t, w_down_bf, xl, lat[5],
                       final_norm if last else None, tm_lat, seq, seq)
        if not last:
            o_da = _da_attn_call(daq_c, [dak_c], [dav_c], [n_ctx], lam_vecs, da_subln[l],
                                 lambda_init, n_ctx, tq_ctx)
            o_gqa = _gqa_attn_call(gq_c, [gk_c], [gv_c], [n_ctx], n_ctx, tq_ctx)
            o_pool = _pool_call(pool_c, pool_w[l], pool_scale[l], n_ctx)
            xc, h2 = _outproj_call(o_da, o_pool, o_gqa, gqa_out_norm[l], w_out_bf, xc, cm[2],
                                   norm_ffn[l], cm[3], cm[4], tm_ctx, nb * n_ctx)
            xc = _ffn_call(h2, w_up_t, conv_w_t, conv_b_t, w_down_bf, xc, cm[5], None,
                           tm_ctx, n_ctx, nb * n_ctx)

    return xl.reshape(nb, seq, d)
```

```python
import functools
import math

import jax
import jax.numpy as jnp
from jax import lax
from jax.experimental import pallas as pl
from jax.experimental.pallas import tpu as pltpu

F32 = jnp.float32
BF16 = jnp.bfloat16

D_MODEL = 2048
GRID_W = 64
EPS = 1e-6
ROPE_BASE = 10000.0
N_MOD = 6
LOG2E = math.log2(math.e)

DA_HEADS = 6
DA_HALF_DIM = 64
HEAD_DIM = 128
DA_WIDTH = DA_HEADS * HEAD_DIM
POOL_WINDOWS = (2, 4, 8, 16)
POOL_WIDTH = len(POOL_WINDOWS) * HEAD_DIM
GQA_Q_HEADS = 6
GQA_KV_HEADS = 2
GQA_GROUP = GQA_Q_HEADS // GQA_KV_HEADS
GQA_WIDTH = GQA_Q_HEADS * HEAD_DIM
GQA_KV_WIDTH = GQA_KV_HEADS * HEAD_DIM
IN_COLS = 3 * DA_WIDTH + POOL_WIDTH + GQA_WIDTH + 2 * GQA_KV_WIDTH
D_FF = 5632
CONV_W = 3

OFF_DAQ = 0
OFF_DAK = OFF_DAQ + DA_WIDTH
OFF_DAV = OFF_DAK + DA_WIDTH
OFF_POOL = OFF_DAV + DA_WIDTH
OFF_GQ = OFF_POOL + POOL_WIDTH
OFF_GK = OFF_GQ + GQA_WIDTH
OFF_GV = OFF_GK + GQA_KV_WIDTH

V7X_VMEM_LIMIT_BYTES = 56 * 1024 * 1024
HALO = 16
FFN_TF = 512
FFN_CHUNK = 512


def _cparams(*sem):
    return pltpu.CompilerParams(dimension_semantics=sem, vmem_limit_bytes=V7X_VMEM_LIMIT_BYTES)


def _dot(a, b):
    return jnp.dot(a, b, preferred_element_type=F32)


def _dot_nt(a, b):
    return lax.dot_general(a, b, (((1,), (1,)), ((), ())), preferred_element_type=F32)


def _rms(x, gain):
    return x * lax.rsqrt(jnp.mean(x * x, axis=-1, keepdims=True) + EPS) * gain


def _modulate(x, gain, shift, scale):
    return _rms(x, gain) * (1.0 + scale) + shift


def _silu(x):
    return x / (1.0 + jnp.exp(-x))


def _mod_kernel(c_ref, w_ref, b_ref, o_ref):
    s = _silu(c_ref[...]).astype(BF16)
    o_ref[0] = _dot(s, w_ref[0].astype(BF16)) + b_ref[0]


def _mod_call(c8, w_mod, b_mod, tn=1024):
    depth, d, n = w_mod.shape
    return pl.pallas_call(
        _mod_kernel,
        grid=(depth, n // tn),
        in_specs=[pl.BlockSpec((8, d), lambda l, j: (0, 0)),
                  pl.BlockSpec((1, d, tn), lambda l, j: (l, 0, j)),
                  pl.BlockSpec((1, 1, tn), lambda l, j: (l, 0, j))],
        out_specs=pl.BlockSpec((1, 8, tn), lambda l, j: (l, 0, j)),
        out_shape=jax.ShapeDtypeStruct((depth, 8, n), F32),
        compiler_params=_cparams("parallel", "parallel"),
        name="adaln_mod",
    )(c8, w_mod, b_mod.reshape(depth, 1, n))


def _rope(x, c, s1, s2, shift):
    return x * c + pltpu.roll(x, HEAD_DIM - shift, 1) * s1 + pltpu.roll(x, shift, 1) * s2


def _inproj_kernel(*refs, use_rope):
    x_ref, sh_ref, sc_ref, nrm_ref, w_ref, qn_ref, kn_ref = refs[:7]
    pos = 7
    if use_rope:
        dac, das1, das2, gc, gs1, gs2 = (r[...] for r in refs[pos:pos + 6])
        pos += 6
    daq_ref, dak_ref, dav_ref, pool_ref, gq_ref, gk_ref, gv_ref = refs[pos:]

    h = _modulate(x_ref[...], nrm_ref[...], sh_ref[0], sc_ref[0]).astype(BF16)

    def proj(off, width):
        return _dot(h, w_ref[:, off:off + width])

    def head(a, i):
        return a[:, i * HEAD_DIM:(i + 1) * HEAD_DIM]

    a = proj(OFF_DAQ, DA_WIDTH)
    for i in range(DA_HEADS):
        t = head(a, i)
        if use_rope:
            t = _rope(t, dac, das1, das2, DA_HALF_DIM // 4)
        daq_ref[:, i * HEAD_DIM:(i + 1) * HEAD_DIM] = (t * (DA_HALF_DIM ** -0.5 * LOG2E)).astype(BF16)
    a = proj(OFF_DAK, DA_WIDTH)
    for i in range(DA_HEADS):
        t = head(a, i)
        if use_rope:
            t = _rope(t, dac, das1, das2, DA_HALF_DIM // 4)
        dak_ref[:, i * HEAD_DIM:(i + 1) * HEAD_DIM] = t.astype(BF16)
    dav_ref[...] = proj(OFF_DAV, DA_WIDTH).astype(BF16)
    pool_ref[...] = proj(OFF_POOL, POOL_WIDTH)
    a = proj(OFF_GQ, GQA_WIDTH)
    for i in range(GQA_Q_HEADS):
        t = _rms(head(a, i), qn_ref[...])
        if use_rope:
            t = _rope(t, gc, gs1, gs2, HEAD_DIM // 4)
        gq_ref[:, i * HEAD_DIM:(i + 1) * HEAD_DIM] = (t * (HEAD_DIM ** -0.5 * LOG2E)).astype(BF16)
    a = proj(OFF_GK, GQA_KV_WIDTH)
    for i in range(GQA_KV_HEADS):
        t = _rms(head(a, i), kn_ref[...])
        if use_rope:
            t = _rope(t, gc, gs1, gs2, HEAD_DIM // 4)
        gk_ref[:, i * HEAD_DIM:(i + 1) * HEAD_DIM] = t.astype(BF16)
    gv_ref[...] = proj(OFF_GV, GQA_KV_WIDTH).astype(BF16)


def _inproj_call(x, shift, scale, norm_gain, w_in_bf, q_norm, k_norm, rope, tm, rows_per_mod):
    m, d = x.shape
    tpm = rows_per_mod // tm
    row = lambda i: (i, 0)
    const = lambda i: (0, 0)
    in_specs = [pl.BlockSpec((tm, d), row),
                pl.BlockSpec((1, 1, d), lambda i: (i // tpm, 0, 0)),
                pl.BlockSpec((1, 1, d), lambda i: (i // tpm, 0, 0)),
                pl.BlockSpec((1, d), const),
                pl.BlockSpec((d, IN_COLS), const),
                pl.BlockSpec((1, HEAD_DIM), const),
                pl.BlockSpec((1, HEAD_DIM), const)]
    args = [x, shift, scale, norm_gain.reshape(1, d), w_in_bf,
            q_norm.reshape(1, HEAD_DIM), k_norm.reshape(1, HEAD_DIM)]
    if rope is not None:
        tps = rope[0].shape[0] // tm
        in_specs += [pl.BlockSpec((tm, HEAD_DIM), lambda i: (i % tps, 0))] * 6
        args += list(rope)
    widths = (DA_WIDTH, DA_WIDTH, DA_WIDTH, POOL_WIDTH, GQA_WIDTH, GQA_KV_WIDTH, GQA_KV_WIDTH)
    dtypes = (BF16, BF16, BF16, F32, BF16, BF16, BF16)
    return pl.pallas_call(
        functools.partial(_inproj_kernel, use_rope=rope is not None),
        grid=(m // tm,),
        in_specs=in_specs,
        out_specs=[pl.BlockSpec((tm, w), row) for w in widths],
        out_shape=[jax.ShapeDtypeStruct((m, w), dt) for w, dt in zip(widths, dtypes)],
        compiler_params=_cparams("parallel"),
        name="in_proj",
    )(*args)


def _attn_kernel(*refs, nseg, tq, tiles_per_iter, lambda_init):
    da = lambda_init is not None
    q_ref = refs[0]
    k_refs = refs[1:1 + nseg]
    v_refs = refs[1 + nseg:1 + 2 * nseg]
    pos = 1 + 2 * nseg
    if da:
        lq1, lk1, lq2, lk2, subln_ref = refs[pos:pos + 5]
        pos += 5
    o_ref, kcat, vcat = refs[pos:]

    off = 0
    for kr, vr in zip(k_refs, v_refs):
        n = kr.shape[0]
        kcat[off:off + n, :] = kr[...]
        vcat[off:off + n, :HEAD_DIM] = vr[...]
        off += n
    vcat[:, HEAD_DIM:] = jnp.ones((off, HEAD_DIM), BF16)

    def attend(q):
        s = _dot_nt(q, kcat[...])
        e = jnp.exp2(s - jnp.max(s, axis=-1, keepdims=True)).astype(BF16)
        o = _dot(e, vcat[...])
        return o[:, :HEAD_DIM] / o[:, HEAD_DIM:]

    if da:
        lam = (jnp.exp(jnp.sum(lq1[...] * lk1[...], axis=-1, keepdims=True))
               - jnp.exp(jnp.sum(lq2[...] * lk2[...], axis=-1, keepdims=True)) + lambda_init)
        subln = subln_ref[...]

    def tile(t):
        rows = pl.ds(pl.multiple_of(t * tq, tq), tq)
        q = q_ref[rows, :]
        if da:
            lane = lax.broadcasted_iota(jnp.int32, q.shape, 1)
            zero = jnp.zeros_like(q)
            a1 = attend(jnp.where(lane < DA_HALF_DIM, q, zero))
            a2 = attend(jnp.where(lane >= DA_HALF_DIM, q, zero))
            o = _rms(a1 - lam * a2, subln) * (1.0 - lambda_init)
        else:
            o = attend(q)
        o_ref[rows, :] = o.astype(o_ref.dtype)

    n_iter = q_ref.shape[0] // (tq * tiles_per_iter)

    def body(it, carry):
        for u in range(tiles_per_iter):
            tile(it * tiles_per_iter + u)
        return carry

    lax.fori_loop(0, n_iter, body, 0)


def _attn_call(q, k_segs, v_segs, seg_lens, sq, tq, tiles_per_iter, n_heads, q_per_kv, out_dtype,
               da_params=None):
    mq = q.shape[0]
    nb = mq // sq
    nseg = len(k_segs)
    sk = sum(seg_lens)
    qspec = pl.BlockSpec((sq, HEAD_DIM), lambda b, h: (b, h))
    kv_specs = [pl.BlockSpec((n, HEAD_DIM), lambda b, h: (b, h // q_per_kv)) for n in seg_lens]
    in_specs = [qspec] + kv_specs + kv_specs
    args = [q, *k_segs, *v_segs]
    lambda_init = None
    if da_params is not None:
        lam_vecs, subln, lambda_init = da_params
        vec = lambda n: pl.BlockSpec((1, n), lambda b, h: (0, 0))
        in_specs += [vec(DA_HALF_DIM)] * 4 + [vec(HEAD_DIM)]
        args += [v.reshape(1, DA_HALF_DIM) for v in lam_vecs] + [subln.reshape(1, HEAD_DIM)]
    return pl.pallas_call(
        functools.partial(_attn_kernel, nseg=nseg, tq=tq, tiles_per_iter=tiles_per_iter,
                          lambda_init=lambda_init),
        grid=(nb, n_heads),
        in_specs=in_specs,
        out_specs=qspec,
        out_shape=jax.ShapeDtypeStruct((mq, n_heads * HEAD_DIM), out_dtype),
        scratch_shapes=[pltpu.VMEM((sk, HEAD_DIM), BF16), pltpu.VMEM((sk, 2 * HEAD_DIM), BF16)],
        compiler_params=_cparams("parallel", "parallel"),
        name="da_attn" if da_params is not None else "gqa_attn",
    )(*args)


def _pool_kernel(u_ref, w_ref, ps_ref, o_ref):
    t = u_ref.shape[0]
    row = lax.broadcasted_iota(jnp.int32, (t, HEAD_DIM), 0)
    for g, w in enumerate(POOL_WINDOWS):
        sl = slice(g * HEAD_DIM, (g + 1) * HEAD_DIM)
        u = u_ref[:, sl]
        acc = jnp.zeros_like(u)
        for d in range(-(w // 2), w - w // 2):
            if d == 0:
                acc = acc + u
                continue
            shifted = pltpu.roll(u, (-d) % t, 0)
            valid = (row >= -d) if d < 0 else (row < t - d)
            acc = acc + jnp.where(valid, shifted, 0.0)
        cnt = jnp.minimum(row - w // 2 + w, t) - jnp.maximum(row - w // 2, 0)
        pooled = acc / cnt.astype(F32) - u
        mixed = _dot(pooled.astype(BF16), w_ref[g].astype(BF16))
        o_ref[:, sl] = (mixed * ps_ref[:, sl]).astype(BF16)


def _pool_call(u, w_pool, pool_scale, seq):
    m = u.shape[0]
    return pl.pallas_call(
        _pool_kernel,
        grid=(m // seq,),
        in_specs=[pl.BlockSpec((seq, POOL_WIDTH), lambda b: (b, 0)),
                  pl.BlockSpec(w_pool.shape, lambda b: (0, 0, 0)),
                  pl.BlockSpec((1, POOL_WIDTH), lambda b: (0, 0))],
        out_specs=pl.BlockSpec((seq, POOL_WIDTH), lambda b: (b, 0)),
        out_shape=jax.ShapeDtypeStruct((m, POOL_WIDTH), BF16),
        compiler_params=_cparams("parallel"),
        name="pool_mixer",
    )(u, w_pool, pool_scale.reshape(1, POOL_WIDTH))


def _outproj_kernel(oda_ref, opool_ref, ogqa_ref, gn_ref, w_ref, x_ref, g1_ref,
                    nrm2_ref, sh2_ref, sc2_ref, xo_ref, h2_ref):
    ogn = _rms(ogqa_ref[...], gn_ref[...]).astype(BF16)
    cat = jnp.concatenate([oda_ref[...], opool_ref[...], ogn], axis=1)
    xn = x_ref[...] + g1_ref[0] * _dot(cat, w_ref[...])
    xo_ref[...] = xn
    h2_ref[...] = _modulate(xn, nrm2_ref[...], sh2_ref[0], sc2_ref[0]).astype(BF16)


def _outproj_call(o_da, o_pool, o_gqa, gqa_out_norm, w_out_bf, x, g1, norm_ffn, sh2, sc2,
                  tm, rows_per_mod):
    m, d = x.shape
    tpm = rows_per_mod // tm
    row = lambda i: (i, 0)
    const = lambda i: (0, 0)
    modrow = pl.BlockSpec((1, 1, d), lambda i: (i // tpm, 0, 0))
    return pl.pallas_call(
        _outproj_kernel,
        grid=(m // tm,),
        in_specs=[pl.BlockSpec((tm, DA_WIDTH), row),
                  pl.BlockSpec((tm, POOL_WIDTH), row),
                  pl.BlockSpec((tm, GQA_WIDTH), row),
                  pl.BlockSpec((1, GQA_WIDTH), const),
                  pl.BlockSpec((d, d), const),
                  pl.BlockSpec((tm, d), row),
                  modrow,
                  pl.BlockSpec((1, d), const),
                  modrow, modrow],
        out_specs=[pl.BlockSpec((tm, d), row), pl.BlockSpec((tm, d), row)],
        out_shape=[jax.ShapeDtypeStruct((m, d), F32), jax.ShapeDtypeStruct((m, d), BF16)],
        compiler_params=_cparams("parallel"),
        name="out_proj",
    )(o_da, o_pool, o_gqa, gqa_out_norm.reshape(1, GQA_WIDTH), w_out_bf, x, g1,
      norm_ffn.reshape(1, d), sh2, sc2)


def _ffn_kernel(*refs, tm, tiles_per_seq, final):
    (h_ref, hp_ref, hn_ref, wg_ref, wv_ref, cwg_ref, cwv_ref, cbg_ref, cbv_ref, wd_ref,
     x_ref, g2_ref) = refs[:12]
    pos = 12
    if final:
        fn_ref = refs[pos]
        pos += 1
    o_ref, lhs_ref, ug_ref, uv_ref = refs[pos:]
    i = pl.program_id(0)
    j = pl.program_id(1)

    @pl.when(j == 0)
    def _():
        first = (i % tiles_per_seq) == 0
        last = (i % tiles_per_seq) == tiles_per_seq - 1
        lhs_ref[0:HALO, :] = jnp.where(first, jnp.zeros_like(hp_ref[...]), hp_ref[...])
        lhs_ref[HALO:HALO + tm, :] = h_ref[...]
        lhs_ref[HALO + tm:, :] = jnp.where(last, jnp.zeros_like(hn_ref[...]), hn_ref[...])
        o_ref[...] = jnp.zeros_like(o_ref)

    lhs = lhs_ref[...]

    def up_conv(w_ref, cw_ref, cb_ref, u_ref, sl):
        u_ref[...] = _dot(lhs, w_ref[:, sl])
        cw = cw_ref[:, sl]
        return (u_ref[pl.ds(HALO - 1, tm), :] * cw[0:1] + u_ref[pl.ds(HALO, tm), :] * cw[1:2]
                + u_ref[pl.ds(HALO + 1, tm), :] * cw[2:3] + cb_ref[:, sl])

    acc = None
    for c in range(wg_ref.shape[1] // FFN_CHUNK):
        sl = slice(c * FFN_CHUNK, (c + 1) * FFN_CHUNK)
        gate = up_conv(wg_ref, cwg_ref, cbg_ref, ug_ref.at[c], sl)
        val = up_conv(wv_ref, cwv_ref, cbv_ref, uv_ref.at[c], sl)
        part = _dot((_silu(gate) * val).astype(BF16), wd_ref[sl, :])
        acc = part if acc is None else acc + part
    o_ref[...] += acc

    @pl.when(j == pl.num_programs(1) - 1)
    def _():
        out = x_ref[...] + g2_ref[0] * o_ref[...]
        if final:
            out = _rms(out, fn_ref[...])
        o_ref[...] = out


def _ffn_call(h2, w_up_bf, conv_w, conv_b, w_down_bf, x, g2, final_norm, tm, seq, rows_per_mod):
    m, d = x.shape
    tf = FFN_TF
    nj = D_FF // tf
    tps = seq // tm
    tpm = rows_per_mod // tm
    hb = tm // HALO
    nhb = m // HALO
    final = final_norm is not None
    row = lambda i, j: (i, 0)
    gate_col = lambda i, j: (0, j)
    val_col = lambda i, j: (0, j + nj)
    in_specs = [pl.BlockSpec((tm, d), row),
                pl.BlockSpec((HALO, d), lambda i, j: (jnp.maximum(i * hb - 1, 0), 0)),
                pl.BlockSpec((HALO, d), lambda i, j: (jnp.minimum((i + 1) * hb, nhb - 1), 0)),
                pl.BlockSpec((d, tf), gate_col),
                pl.BlockSpec((d, tf), val_col),
                pl.BlockSpec((CONV_W, tf), gate_col),
                pl.BlockSpec((CONV_W, tf), val_col),
                pl.BlockSpec((1, tf), gate_col),
                pl.BlockSpec((1, tf), val_col),
                pl.BlockSpec((tf, d), lambda i, j: (j, 0)),
                pl.BlockSpec((tm, d), row),
                pl.BlockSpec((1, 1, d), lambda i, j: (i // tpm, 0, 0))]
    cb = conv_b.reshape(1, 2 * D_FF)
    args = [h2, h2, h2, w_up_bf, w_up_bf, conv_w, conv_w, cb, cb, w_down_bf, x, g2]
    if final:
        in_specs.append(pl.BlockSpec((1, d), lambda i, j: (0, 0)))
        args.append(final_norm.reshape(1, d))
    return pl.pallas_call(
        functools.partial(_ffn_kernel, tm=tm, tiles_per_seq=tps, final=final),
        grid=(m // tm, nj),
        in_specs=in_specs,
        out_specs=pl.BlockSpec((tm, d), row),
        out_shape=jax.ShapeDtypeStruct((m, d), F32),
        scratch_shapes=[pltpu.VMEM((tm + 2 * HALO, d), BF16),
                        pltpu.VMEM((tf // FFN_CHUNK, tm + 2 * HALO, FFN_CHUNK), F32),
                        pltpu.VMEM((tf // FFN_CHUNK, tm + 2 * HALO, FFN_CHUNK), F32)],
        compiler_params=_cparams("parallel", "arbitrary"),
        name="conv_ffn",
    )(*args)


def _rope_tables(seq, rot_dim):
    rows = seq // GRID_W
    row = jnp.repeat(jnp.arange(rows, dtype=F32), GRID_W)
    col = jnp.tile(jnp.arange(GRID_W, dtype=F32), rows)
    axis_dim = rot_dim // 2
    nf = axis_dim // 2
    freqs = ROPE_BASE ** (-(jnp.arange(nf, dtype=F32) * 2.0 / axis_dim))
    ar, ac = row[:, None] * freqs, col[:, None] * freqs
    cos = jnp.concatenate([jnp.cos(ar)] * 2 + [jnp.cos(ac)] * 2, axis=-1)
    sin = jnp.concatenate([jnp.sin(ar)] * 2 + [jnp.sin(ac)] * 2, axis=-1)
    first = jnp.tile(jnp.concatenate([jnp.ones((nf,), F32), jnp.zeros((nf,), F32)]), 2)
    reps = HEAD_DIM // rot_dim
    return (jnp.tile(cos, (1, reps)), jnp.tile(-sin * first, (1, reps)),
            jnp.tile(sin * (1.0 - first), (1, reps)))


def kernel(x, c, ctx, c_ctx, w_mod, b_mod, norm_mix, norm_ffn, w_in, da_lambda_q1, da_lambda_k1,
           da_lambda_q2, da_lambda_k2, da_subln, gqa_q_norm, gqa_k_norm, pool_w, pool_scale,
           gqa_out_norm, w_out, w_up, conv_w, conv_b, w_down, final_norm):
    nb, seq, d = x.shape
    n_ctx = ctx.shape[1]
    depth = w_mod.shape[0]
    assert d == D_MODEL and seq % GRID_W == 0 and nb <= 7

    tm_lat = min(512, seq)
    tm_ctx = min(256, n_ctx)
    tq_lat = min(256, seq)
    tq_ctx = min(256, n_ctx)
    tpi_lat = math.gcd(seq // tq_lat, 8)

    c8 = jnp.zeros((8, d), F32).at[:nb].set(c).at[nb].set(c_ctx)
    mods = _mod_call(c8, w_mod, b_mod)

    rope = _rope_tables(seq, DA_HALF_DIM) + _rope_tables(seq, HEAD_DIM)

    xl = x.reshape(nb * seq, d)
    xc = ctx.reshape(nb * n_ctx, d)

    for l in range(depth):
        last = l == depth - 1
        lambda_init = 0.8 - 0.6 * math.exp(-0.3 * l)
        m = mods[l]
        lat = [m[:nb, k * d:(k + 1) * d].reshape(nb, 1, d) for k in range(N_MOD)]
        cm = [m[nb:nb + 1, k * d:(k + 1) * d].reshape(1, 1, d) for k in range(N_MOD)]
        da_params = ((da_lambda_q1[l], da_lambda_k1[l], da_lambda_q2[l], da_lambda_k2[l]),
                     da_subln[l], lambda_init)
        w_in_bf = w_in[l].astype(BF16)
        w_out_bf = w_out[l].astype(BF16)
        ffn_w = (w_up[l].astype(BF16), conv_w[l], conv_b[l], w_down[l].astype(BF16))

        pl_ = _inproj_call(xl, lat[0], lat[1], norm_mix[l], w_in_bf, gqa_q_norm[l], gqa_k_norm[l],
                           rope, tm_lat, seq)
        pc_ = _inproj_call(xc, cm[0], cm[1], norm_mix[l], w_in_bf, gqa_q_norm[l], gqa_k_norm[l],
                           None, tm_ctx, nb * n_ctx)
        daq_l, dak_l, dav_l, pool_l, gq_l, gk_l, gv_l = pl_
        daq_c, dak_c, dav_c, pool_c, gq_c, gk_c, gv_c = pc_

        o_da = _attn_call(daq_l, [dak_c, dak_l], [dav_c, dav_l], [n_ctx, seq], seq, tq_lat, tpi_lat,
                          DA_HEADS, 1, BF16, da_params)
        o_gqa = _attn_call(gq_l, [gk_c, gk_l], [gv_c, gv_l], [n_ctx, seq], seq, tq_lat, tpi_lat,
                           GQA_Q_HEADS, GQA_GROUP, F32)
        o_pool = _pool_call(pool_l, pool_w[l], pool_scale[l], seq)
        xl, h2 = _outproj_call(o_da, o_pool, o_gqa, gqa_out_norm[l], w_out_bf, xl, lat[2],
                               norm_ffn[l], lat[3], lat[4], tm_lat, seq)
        xl = _ffn_call(h2, *ffn_w, xl, lat[5], final_norm if last else None, tm_lat, seq, seq)
        if not last:
            o_da = _attn_call(daq_c, [dak_c], [dav_c], [n_ctx], n_ctx, tq_ctx, 1,
                              DA_HEADS, 1, BF16, da_params)
            o_gqa = _attn_call(gq_c, [gk_c], [gv_c], [n_ctx], n_ctx, tq_ctx, 1,
                               GQA_Q_HEADS, GQA_GROUP, F32)
            o_pool = _pool_call(pool_c, pool_w[l], pool_scale[l], n_ctx)
            xc, h2 = _outproj_call(o_da, o_pool, o_gqa, gqa_out_norm[l], w_out_bf, xc, cm[2],
                                   norm_ffn[l], cm[3], cm[4], tm_ctx, nb * n_ctx)
            xc = _ffn_call(h2, *ffn_w, xc, cm[5], None, tm_ctx, n_ctx, nb * n_ctx)

    return xl.reshape(nb, seq, d)
```

```python
import functools
import math

import jax
import jax.numpy as jnp
from jax import lax
from jax.experimental import pallas as pl
from jax.experimental.pallas import tpu as pltpu

F32 = jnp.float32
BF16 = jnp.bfloat16

D_MODEL = 2048
GRID_W = 64
EPS = 1e-6
ROPE_BASE = 10000.0
N_MOD = 6
LOG2E = math.log2(math.e)

DA_HEADS = 6
DA_HALF_DIM = 64
HEAD_DIM = 128
DA_WIDTH = DA_HEADS * HEAD_DIM
POOL_WINDOWS = (2, 4, 8, 16)
POOL_WIDTH = len(POOL_WINDOWS) * HEAD_DIM
GQA_Q_HEADS = 6
GQA_KV_HEADS = 2
GQA_GROUP = GQA_Q_HEADS // GQA_KV_HEADS
GQA_WIDTH = GQA_Q_HEADS * HEAD_DIM
GQA_KV_WIDTH = GQA_KV_HEADS * HEAD_DIM
IN_COLS = 3 * DA_WIDTH + POOL_WIDTH + GQA_WIDTH + 2 * GQA_KV_WIDTH
D_FF = 5632
CONV_W = 3

OFF_DAQ = 0
OFF_DAK = OFF_DAQ + DA_WIDTH
OFF_DAV = OFF_DAK + DA_WIDTH
OFF_POOL = OFF_DAV + DA_WIDTH
OFF_GQ = OFF_POOL + POOL_WIDTH
OFF_GK = OFF_GQ + GQA_WIDTH
OFF_GV = OFF_GK + GQA_KV_WIDTH

V7X_VMEM_LIMIT_BYTES = 56 * 1024 * 1024
HALO = 16
FFN_TF = 512
PROJ_SUB = 256
CAST_STEPS = 16


def _cparams(*sem):
    return pltpu.CompilerParams(dimension_semantics=sem, vmem_limit_bytes=V7X_VMEM_LIMIT_BYTES)


def _dot(a, b):
    return jnp.dot(a, b, preferred_element_type=F32)


def _dot_nt(a, b):
    return lax.dot_general(a, b, (((1,), (1,)), ((), ())), preferred_element_type=F32)


def _rms(x, gain):
    return x * lax.rsqrt(jnp.mean(x * x, axis=-1, keepdims=True) + EPS) * gain


def _modulate(x, gain, shift, scale):
    return _rms(x, gain) * (1.0 + scale) + shift


def _silu(x):
    return x / (1.0 + jnp.exp(-x))


def _cast_specs(jobs, step_of, grid_steps):
    in_specs, out_specs, out_shapes, args = [], [], [], []
    n = min(CAST_STEPS, 1 << (grid_steps.bit_length() - 1))
    for w, layer in jobs:
        _, r, c = w.shape
        rb = r // n
        assert rb * n == r and rb % HALO == 0
        blk = lambda *g: jnp.minimum(step_of(*g), n - 1)
        in_specs.append(pl.BlockSpec((None, rb, c), lambda *g, layer=layer, blk=blk: (layer, blk(*g), 0)))
        out_specs.append(pl.BlockSpec((rb, c), lambda *g, blk=blk: (blk(*g), 0)))
        out_shapes.append(jax.ShapeDtypeStruct((r, c), BF16))
        args.append(w)
    return in_specs, out_specs, out_shapes, args


def _run_casts(in_refs, out_refs):
    for src, dst in zip(in_refs, out_refs):
        dst[...] = src[...].astype(BF16)


def _mod_kernel(*refs, n_cast):
    c_ref, w_ref, b_ref = refs[:3]
    cast_in = refs[3:3 + n_cast]
    o_ref = refs[3 + n_cast]
    cast_out = refs[4 + n_cast:]
    s = _silu(c_ref[...]).astype(BF16)
    o_ref[0] = _dot(s, w_ref[0].astype(BF16)) + b_ref[0]
    _run_casts(cast_in, cast_out)


def _mod_call(c8, w_mod, b_mod, cast_jobs, tn=1024):
    depth, d, n = w_mod.shape
    nj = n // tn
    c_in, c_out, c_shapes, c_args = _cast_specs(cast_jobs, lambda l, j: l * nj + j, depth * nj)
    return pl.pallas_call(
        functools.partial(_mod_kernel, n_cast=len(cast_jobs)),
        grid=(depth, nj),
        in_specs=[pl.BlockSpec((8, d), lambda l, j: (0, 0)),
                  pl.BlockSpec((1, d, tn), lambda l, j: (l, 0, j)),
                  pl.BlockSpec((1, 1, tn), lambda l, j: (l, 0, j))] + c_in,
        out_specs=[pl.BlockSpec((1, 8, tn), lambda l, j: (l, 0, j))] + c_out,
        out_shape=[jax.ShapeDtypeStruct((depth, 8, n), F32)] + c_shapes,
        compiler_params=_cparams("arbitrary", "arbitrary"),
        name="adaln_mod",
    )(c8, w_mod, b_mod.reshape(depth, 1, n), *c_args)


def _rope(x, c, s1, s2, shift):
    return x * c + pltpu.roll(x, HEAD_DIM - shift, 1) * s1 + pltpu.roll(x, shift, 1) * s2


def _inproj_kernel(*refs, use_rope, sub):
    x_ref, sh_ref, sc_ref, nrm_ref, w_ref, qn_ref, kn_ref = refs[:7]
    pos = 7
    rope_refs = ()
    if use_rope:
        rope_refs = refs[pos:pos + 6]
        pos += 6
    daq_ref, dak_ref, dav_ref, pool_ref, gq_ref, gk_ref, gv_ref = refs[pos:]

    def head_cols(i):
        return slice(i * HEAD_DIM, (i + 1) * HEAD_DIM)

    for r in range(x_ref.shape[0] // sub):
        rows = slice(r * sub, (r + 1) * sub)
        h = _modulate(x_ref[rows, :], nrm_ref[...], sh_ref[0], sc_ref[0]).astype(BF16)
        dac, das1, das2, gc, gs1, gs2 = (t[rows, :] for t in rope_refs) if use_rope else (None,) * 6

        def proj(off, width):
            return _dot(h, w_ref[:, off:off + width])

        a = proj(OFF_DAQ, DA_WIDTH)
        for i in range(DA_HEADS):
            t = a[:, head_cols(i)]
            if use_rope:
                t = _rope(t, dac, das1, das2, DA_HALF_DIM // 4)
            daq_ref[rows, head_cols(i)] = (t * (DA_HALF_DIM ** -0.5 * LOG2E)).astype(BF16)
        a = proj(OFF_DAK, DA_WIDTH)
        for i in range(DA_HEADS):
            t = a[:, head_cols(i)]
            if use_rope:
                t = _rope(t, dac, das1, das2, DA_HALF_DIM // 4)
            dak_ref[rows, head_cols(i)] = t.astype(BF16)
        dav_ref[rows, :] = proj(OFF_DAV, DA_WIDTH).astype(BF16)
        pool_ref[rows, :] = proj(OFF_POOL, POOL_WIDTH)
        a = proj(OFF_GQ, GQA_WIDTH)
        for i in range(GQA_Q_HEADS):
            t = _rms(a[:, head_cols(i)], qn_ref[...])
            if use_rope:
                t = _rope(t, gc, gs1, gs2, HEAD_DIM // 4)
            gq_ref[rows, head_cols(i)] = (t * (HEAD_DIM ** -0.5 * LOG2E)).astype(BF16)
        a = proj(OFF_GK, GQA_KV_WIDTH)
        for i in range(GQA_KV_HEADS):
            t = _rms(a[:, head_cols(i)], kn_ref[...])
            if use_rope:
                t = _rope(t, gc, gs1, gs2, HEAD_DIM // 4)
            gk_ref[rows, head_cols(i)] = t.astype(BF16)
        gv_ref[rows, :] = proj(OFF_GV, GQA_KV_WIDTH).astype(BF16)


def _inproj_call(x, shift, scale, norm_gain, w_in_bf, q_norm, k_norm, rope, tm, rows_per_mod):
    m, d = x.shape
    tpm = rows_per_mod // tm
    row = lambda i: (i, 0)
    const = lambda i: (0, 0)
    in_specs = [pl.BlockSpec((tm, d), row),
                pl.BlockSpec((1, 1, d), lambda i: (i // tpm, 0, 0)),
                pl.BlockSpec((1, 1, d), lambda i: (i // tpm, 0, 0)),
                pl.BlockSpec((1, d), const),
                pl.BlockSpec((d, IN_COLS), const, pipeline_mode=pl.Buffered(1)),
                pl.BlockSpec((1, HEAD_DIM), const),
                pl.BlockSpec((1, HEAD_DIM), const)]
    args = [x, shift, scale, norm_gain.reshape(1, d), w_in_bf,
            q_norm.reshape(1, HEAD_DIM), k_norm.reshape(1, HEAD_DIM)]
    if rope is not None:
        tps = rope[0].shape[0] // tm
        in_specs += [pl.BlockSpec((tm, HEAD_DIM), lambda i: (i % tps, 0))] * 6
        args += list(rope)
    widths = (DA_WIDTH, DA_WIDTH, DA_WIDTH, POOL_WIDTH, GQA_WIDTH, GQA_KV_WIDTH, GQA_KV_WIDTH)
    dtypes = (BF16, BF16, BF16, F32, BF16, BF16, BF16)
    return pl.pallas_call(
        functools.partial(_inproj_kernel, use_rope=rope is not None, sub=min(tm, PROJ_SUB)),
        grid=(m // tm,),
        in_specs=in_specs,
        out_specs=[pl.BlockSpec((tm, w), row) for w in widths],
        out_shape=[jax.ShapeDtypeStruct((m, w), dt) for w, dt in zip(widths, dtypes)],
        compiler_params=_cparams("parallel"),
        name="in_proj",
    )(*args)


def _attn_kernel(*refs, nseg, tq, tiles_per_iter, lambda_init, n_cast):
    da = lambda_init is not None
    q_ref = refs[0]
    k_refs = refs[1:1 + nseg]
    v_refs = refs[1 + nseg:1 + 2 * nseg]
    pos = 1 + 2 * nseg
    if da:
        lq1, lk1, lq2, lk2, subln_ref = refs[pos:pos + 5]
        pos += 5
    cast_in = refs[pos:pos + n_cast]
    o_ref = refs[pos + n_cast]
    cast_out = refs[pos + n_cast + 1:pos + 2 * n_cast + 1]
    kcat, vcat = refs[pos + 2 * n_cast + 1:]
    _run_casts(cast_in, cast_out)

    off = 0
    for kr, vr in zip(k_refs, v_refs):
        n = kr.shape[0]
        kcat[off:off + n, :] = kr[...]
        vcat[off:off + n, :HEAD_DIM] = vr[...]
        off += n
    vcat[:, HEAD_DIM:] = jnp.ones((off, HEAD_DIM), BF16)

    def attend(q):
        s = _dot_nt(q, kcat[...])
        e = jnp.exp2(s - jnp.max(s, axis=-1, keepdims=True)).astype(BF16)
        o = _dot(e, vcat[...])
        return o[:, :HEAD_DIM] / o[:, HEAD_DIM:]

    if da:
        lam = (jnp.exp(jnp.sum(lq1[...] * lk1[...], axis=-1, keepdims=True))
               - jnp.exp(jnp.sum(lq2[...] * lk2[...], axis=-1, keepdims=True)) + lambda_init)
        subln = subln_ref[...]

    def tile(t):
        rows = pl.ds(pl.multiple_of(t * tq, tq), tq)
        q = q_ref[rows, :]
        if da:
            lane = lax.broadcasted_iota(jnp.int32, q.shape, 1)
            zero = jnp.zeros_like(q)
            a1 = attend(jnp.where(lane < DA_HALF_DIM, q, zero))
            a2 = attend(jnp.where(lane >= DA_HALF_DIM, q, zero))
            o = _rms(a1 - lam * a2, subln) * (1.0 - lambda_init)
        else:
            o = attend(q)
        o_ref[rows, :] = o.astype(o_ref.dtype)

    n_iter = q_ref.shape[0] // (tq * tiles_per_iter)

    def body(it, carry):
        for u in range(tiles_per_iter):
            tile(it * tiles_per_iter + u)
        return carry

    lax.fori_loop(0, n_iter, body, 0)


def _attn_call(q, k_segs, v_segs, seg_lens, sq, tq, tiles_per_iter, n_heads, q_per_kv, out_dtype,
               da_params=None, cast_jobs=()):
    mq = q.shape[0]
    nb = mq // sq
    nseg = len(k_segs)
    sk = sum(seg_lens)
    c_in, c_out, c_shapes, c_args = _cast_specs(cast_jobs, lambda b, h: b * n_heads + h, nb * n_heads)
    qspec = pl.BlockSpec((sq, HEAD_DIM), lambda b, h: (b, h))
    kv_specs = [pl.BlockSpec((n, HEAD_DIM), lambda b, h: (b, h // q_per_kv)) for n in seg_lens]
    in_specs = [qspec] + kv_specs + kv_specs
    args = [q, *k_segs, *v_segs]
    lambda_init = None
    if da_params is not None:
        lam_vecs, subln, lambda_init = da_params
        vec = lambda n: pl.BlockSpec((1, n), lambda b, h: (0, 0))
        in_specs += [vec(DA_HALF_DIM)] * 4 + [vec(HEAD_DIM)]
        args += [v.reshape(1, DA_HALF_DIM) for v in lam_vecs] + [subln.reshape(1, HEAD_DIM)]
    return pl.pallas_call(
        functools.partial(_attn_kernel, nseg=nseg, tq=tq, tiles_per_iter=tiles_per_iter,
                          lambda_init=lambda_init, n_cast=len(cast_jobs)),
        grid=(nb, n_heads),
        in_specs=in_specs + c_in,
        out_specs=[qspec] + c_out,
        out_shape=[jax.ShapeDtypeStruct((mq, n_heads * HEAD_DIM), out_dtype)] + c_shapes,
        scratch_shapes=[pltpu.VMEM((sk, HEAD_DIM), BF16), pltpu.VMEM((sk, 2 * HEAD_DIM), BF16)],
        compiler_params=_cparams("arbitrary", "arbitrary"),
        name="da_attn" if da_params is not None else "gqa_attn",
    )(*args, *c_args)


def _pool_kernel(u_ref, w_ref, ps_ref, o_ref):
    t = u_ref.shape[0]
    row = lax.broadcasted_iota(jnp.int32, (t, HEAD_DIM), 0)
    for g, w in enumerate(POOL_WINDOWS):
        sl = slice(g * HEAD_DIM, (g + 1) * HEAD_DIM)
        u = u_ref[:, sl]
        acc = jnp.zeros_like(u)
        for d in range(-(w // 2), w - w // 2):
            if d == 0:
                acc = acc + u
                continue
            shifted = pltpu.roll(u, (-d) % t, 0)
            valid = (row >= -d) if d < 0 else (row < t - d)
            acc = acc + jnp.where(valid, shifted, 0.0)
        cnt = jnp.minimum(row - w // 2 + w, t) - jnp.maximum(row - w // 2, 0)
        pooled = acc / cnt.astype(F32) - u
        mixed = _dot(pooled.astype(BF16), w_ref[g].astype(BF16))
        o_ref[:, sl] = (mixed * ps_ref[:, sl]).astype(BF16)


def _pool_call(u, w_pool, pool_scale, seq):
    m = u.shape[0]
    return pl.pallas_call(
        _pool_kernel,
        grid=(m // seq,),
        in_specs=[pl.BlockSpec((seq, POOL_WIDTH), lambda b: (b, 0)),
                  pl.BlockSpec(w_pool.shape, lambda b: (0, 0, 0)),
                  pl.BlockSpec((1, POOL_WIDTH), lambda b: (0, 0))],
        out_specs=pl.BlockSpec((seq, POOL_WIDTH), lambda b: (b, 0)),
        out_shape=jax.ShapeDtypeStruct((m, POOL_WIDTH), BF16),
        compiler_params=_cparams("parallel"),
        name="pool_mixer",
    )(u, w_pool, pool_scale.reshape(1, POOL_WIDTH))


def _outproj_kernel(oda_ref, opool_ref, ogqa_ref, gn_ref, w_ref, x_ref, g1_ref,
                    nrm2_ref, sh2_ref, sc2_ref, xo_ref, h2_ref, *, sub):
    for r in range(x_ref.shape[0] // sub):
        rows = slice(r * sub, (r + 1) * sub)
        ogn = _rms(ogqa_ref[rows, :], gn_ref[...]).astype(BF16)
        cat = jnp.concatenate([oda_ref[rows, :], opool_ref[rows, :], ogn], axis=1)
        xn = x_ref[rows, :] + g1_ref[0] * _dot(cat, w_ref[...])
        xo_ref[rows, :] = xn
        h2_ref[rows, :] = _modulate(xn, nrm2_ref[...], sh2_ref[0], sc2_ref[0]).astype(BF16)


def _outproj_call(o_da, o_pool, o_gqa, gqa_out_norm, w_out_bf, x, g1, norm_ffn, sh2, sc2,
                  tm, rows_per_mod):
    m, d = x.shape
    tpm = rows_per_mod // tm
    row = lambda i: (i, 0)
    const = lambda i: (0, 0)
    modrow = pl.BlockSpec((1, 1, d), lambda i: (i // tpm, 0, 0))
    return pl.pallas_call(
        functools.partial(_outproj_kernel, sub=min(tm, PROJ_SUB)),
        grid=(m // tm,),
        in_specs=[pl.BlockSpec((tm, DA_WIDTH), row),
                  pl.BlockSpec((tm, POOL_WIDTH), row),
                  pl.BlockSpec((tm, GQA_WIDTH), row),
                  pl.BlockSpec((1, GQA_WIDTH), const),
                  pl.BlockSpec((d, d), const, pipeline_mode=pl.Buffered(1)),
                  pl.BlockSpec((tm, d), row),
                  modrow,
                  pl.BlockSpec((1, d), const),
                  modrow, modrow],
        out_specs=[pl.BlockSpec((tm, d), row), pl.BlockSpec((tm, d), row)],
        out_shape=[jax.ShapeDtypeStruct((m, d), F32), jax.ShapeDtypeStruct((m, d), BF16)],
        compiler_params=_cparams("parallel"),
        name="out_proj",
    )(o_da, o_pool, o_gqa, gqa_out_norm.reshape(1, GQA_WIDTH), w_out_bf, x, g1,
      norm_ffn.reshape(1, d), sh2, sc2)


def _ffn_kernel(*refs, tm, tiles_per_seq, final):
    (h_ref, hp_ref, hn_ref, wg_ref, wv_ref, cwg_ref, cwv_ref, cbg_ref, cbv_ref, wd_ref,
     x_ref, g2_ref) = refs[:12]
    pos = 12
    if final:
        fn_ref = refs[pos]
        pos += 1
    o_ref, lhs_ref, ug_ref, uv_ref = refs[pos:]
    i = pl.program_id(0)
    j = pl.program_id(1)

    @pl.when(j == 0)
    def _():
        first = (i % tiles_per_seq) == 0
        last = (i % tiles_per_seq) == tiles_per_seq - 1
        lhs_ref[0:HALO, :] = jnp.where(first, jnp.zeros_like(hp_ref[...]), hp_ref[...])
        lhs_ref[HALO:HALO + tm, :] = h_ref[...]
        lhs_ref[HALO + tm:, :] = jnp.where(last, jnp.zeros_like(hn_ref[...]), hn_ref[...])
        o_ref[...] = jnp.zeros_like(o_ref)

    lhs = lhs_ref[...]

    def up_conv(w_ref, cw_ref, cb_ref, u_ref):
        u_ref[...] = _dot(lhs, w_ref[...])
        cw = cw_ref[...]
        return (u_ref[pl.ds(HALO - 1, tm), :] * cw[0:1] + u_ref[pl.ds(HALO, tm), :] * cw[1:2]
                + u_ref[pl.ds(HALO + 1, tm), :] * cw[2:3] + cb_ref[...])

    gate = up_conv(wg_ref, cwg_ref, cbg_ref, ug_ref)
    val = up_conv(wv_ref, cwv_ref, cbv_ref, uv_ref)
    o_ref[...] += _dot((_silu(gate) * val).astype(BF16), wd_ref[...])

    @pl.when(j == pl.num_programs(1) - 1)
    def _():
        out = x_ref[...] + g2_ref[0] * o_ref[...]
        if final:
            out = _rms(out, fn_ref[...])
        o_ref[...] = out


def _ffn_call(h2, w_up_bf, conv_w, conv_b, w_down_bf, x, g2, final_norm, tm, seq, rows_per_mod):
    m, d = x.shape
    tf = FFN_TF
    nj = D_FF // tf
    tps = seq // tm
    tpm = rows_per_mod // tm
    hb = tm // HALO
    nhb = m // HALO
    final = final_norm is not None
    row = lambda i, j: (i, 0)
    gate_col = lambda i, j: (0, j)
    val_col = lambda i, j: (0, j + nj)
    in_specs = [pl.BlockSpec((tm, d), row),
                pl.BlockSpec((HALO, d), lambda i, j: (jnp.maximum(i * hb - 1, 0), 0)),
                pl.BlockSpec((HALO, d), lambda i, j: (jnp.minimum((i + 1) * hb, nhb - 1), 0)),
                pl.BlockSpec((d, tf), gate_col),
                pl.BlockSpec((d, tf), val_col),
                pl.BlockSpec((CONV_W, tf), gate_col),
                pl.BlockSpec((CONV_W, tf), val_col),
                pl.BlockSpec((1, tf), gate_col),
                pl.BlockSpec((1, tf), val_col),
                pl.BlockSpec((tf, d), lambda i, j: (j, 0)),
                pl.BlockSpec((tm, d), row),
                pl.BlockSpec((1, 1, d), lambda i, j: (i // tpm, 0, 0))]
    cb = conv_b.reshape(1, 2 * D_FF)
    args = [h2, h2, h2, w_up_bf, w_up_bf, conv_w, conv_w, cb, cb, w_down_bf, x, g2]
    if final:
        in_specs.append(pl.BlockSpec((1, d), lambda i, j: (0, 0)))
        args.append(final_norm.reshape(1, d))
    return pl.pallas_call(
        functools.partial(_ffn_kernel, tm=tm, tiles_per_seq=tps, final=final),
        grid=(m // tm, nj),
        in_specs=in_specs,
        out_specs=pl.BlockSpec((tm, d), row),
        out_shape=jax.ShapeDtypeStruct((m, d), F32),
        scratch_shapes=[pltpu.VMEM((tm + 2 * HALO, d), BF16),
                        pltpu.VMEM((tm + 2 * HALO, tf), F32),
                        pltpu.VMEM((tm + 2 * HALO, tf), F32)],
        compiler_params=_cparams("parallel", "arbitrary"),
        name="conv_ffn",
    )(*args)


def _rope_tables(seq, rot_dim):
    rows = seq // GRID_W
    row = jnp.repeat(jnp.arange(rows, dtype=F32), GRID_W)
    col = jnp.tile(jnp.arange(GRID_W, dtype=F32), rows)
    axis_dim = rot_dim // 2
    nf = axis_dim // 2
    freqs = ROPE_BASE ** (-(jnp.arange(nf, dtype=F32) * 2.0 / axis_dim))
    ar, ac = row[:, None] * freqs, col[:, None] * freqs
    cos = jnp.concatenate([jnp.cos(ar)] * 2 + [jnp.cos(ac)] * 2, axis=-1)
    sin = jnp.concatenate([jnp.sin(ar)] * 2 + [jnp.sin(ac)] * 2, axis=-1)
    first = jnp.tile(jnp.concatenate([jnp.ones((nf,), F32), jnp.zeros((nf,), F32)]), 2)
    reps = HEAD_DIM // rot_dim
    return (jnp.tile(cos, (1, reps)), jnp.tile(-sin * first, (1, reps)),
            jnp.tile(sin * (1.0 - first), (1, reps)))


def kernel(x, c, ctx, c_ctx, w_mod, b_mod, norm_mix, norm_ffn, w_in, da_lambda_q1, da_lambda_k1,
           da_lambda_q2, da_lambda_k2, da_subln, gqa_q_norm, gqa_k_norm, pool_w, pool_scale,
           gqa_out_norm, w_out, w_up, conv_w, conv_b, w_down, final_norm):
    nb, seq, d = x.shape
    n_ctx = ctx.shape[1]
    depth = w_mod.shape[0]
    assert d == D_MODEL and seq % GRID_W == 0 and nb <= 7

    tm_lat = min(512, seq)
    tm_ctx = min(256, n_ctx)
    tq_lat = min(256, seq)
    tq_ctx = min(256, n_ctx)
    tpi_lat = math.gcd(seq // tq_lat, 8)

    c8 = jnp.zeros((8, d), F32).at[:nb].set(c).at[nb].set(c_ctx)
    mods, w_in_bf = _mod_call(c8, w_mod, b_mod, [(w_in, 0)])

    rope = _rope_tables(seq, DA_HALF_DIM) + _rope_tables(seq, HEAD_DIM)

    xl = x.reshape(nb * seq, d)
    xc = ctx.reshape(nb * n_ctx, d)

    for l in range(depth):
        last = l == depth - 1
        lambda_init = 0.8 - 0.6 * math.exp(-0.3 * l)
        m = mods[l]
        lat = [m[:nb, k * d:(k + 1) * d].reshape(nb, 1, d) for k in range(N_MOD)]
        cm = [m[nb:nb + 1, k * d:(k + 1) * d].reshape(1, 1, d) for k in range(N_MOD)]
        da_params = ((da_lambda_q1[l], da_lambda_k1[l], da_lambda_q2[l], da_lambda_k2[l]),
                     da_subln[l], lambda_init)

        pl_ = _inproj_call(xl, lat[0], lat[1], norm_mix[l], w_in_bf, gqa_q_norm[l], gqa_k_norm[l],
                           rope, tm_lat, seq)
        pc_ = _inproj_call(xc, cm[0], cm[1], norm_mix[l], w_in_bf, gqa_q_norm[l], gqa_k_norm[l],
                           None, tm_ctx, nb * n_ctx)
        daq_l, dak_l, dav_l, pool_l, gq_l, gk_l, gv_l = pl_
        daq_c, dak_c, dav_c, pool_c, gq_c, gk_c, gv_c = pc_

        o_da, w_up_bf, w_down_bf = _attn_call(
            daq_l, [dak_c, dak_l], [dav_c, dav_l], [n_ctx, seq], seq, tq_lat, tpi_lat,
            DA_HEADS, 1, BF16, da_params, cast_jobs=[(w_up, l), (w_down, l)])
        o_gqa, w_out_bf, *w_in_next = _attn_call(
            gq_l, [gk_c, gk_l], [gv_c, gv_l], [n_ctx, seq], seq, tq_lat, tpi_lat,
            GQA_Q_HEADS, GQA_GROUP, F32, cast_jobs=[(w_out, l)] + ([] if last else [(w_in, l + 1)]))
        ffn_w = (w_up_bf, conv_w[l], conv_b[l], w_down_bf)
        o_pool = _pool_call(pool_l, pool_w[l], pool_scale[l], seq)
        xl, h2 = _outproj_call(o_da, o_pool, o_gqa, gqa_out_norm[l], w_out_bf, xl, lat[2],
                               norm_ffn[l], lat[3], lat[4], tm_lat, seq)
        xl = _ffn_call(h2, *ffn_w, xl, lat[5], final_norm if last else None, tm_lat, seq, seq)
        if not last:
            o_da, = _attn_call(daq_c, [dak_c], [dav_c], [n_ctx], n_ctx, tq_ctx, 1,
                               DA_HEADS, 1, BF16, da_params)
            o_gqa, = _attn_call(gq_c, [gk_c], [gv_c], [n_ctx], n_ctx, tq_ctx, 1,
                                GQA_Q_HEADS, GQA_GROUP, F32)
            o_pool = _pool_call(pool_c, pool_w[l], pool_scale[l], n_ctx)
            xc, h2 = _outproj_call(o_da, o_pool, o_gqa, gqa_out_norm[l], w_out_bf, xc, cm[2],
                                   norm_ffn[l], cm[3], cm[4], tm_ctx, nb * n_ctx)
            xc = _ffn_call(h2, *ffn_w, xc, cm[5], None, tm_ctx, n_ctx, nb * n_ctx)
            w_in_bf, = w_in_next

    return xl.reshape(nb, seq, d)
```

```python
import functools
import math

import jax
import jax.numpy as jnp
from jax import lax
from jax.experimental import pallas as pl
from jax.experimental.pallas import tpu as pltpu

F32 = jnp.float32
BF16 = jnp.bfloat16

D_MODEL = 2048
GRID_W = 64
EPS = 1e-6
ROPE_BASE = 10000.0
N_MOD = 6
LOG2E = math.log2(math.e)

DA_HEADS = 6
DA_HALF_DIM = 64
HEAD_DIM = 128
DA_WIDTH = DA_HEADS * HEAD_DIM
POOL_WINDOWS = (2, 4, 8, 16)
POOL_WIDTH = len(POOL_WINDOWS) * HEAD_DIM
GQA_Q_HEADS = 6
GQA_KV_HEADS = 2
GQA_GROUP = GQA_Q_HEADS // GQA_KV_HEADS
GQA_WIDTH = GQA_Q_HEADS * HEAD_DIM
GQA_KV_WIDTH = GQA_KV_HEADS * HEAD_DIM
IN_COLS = 3 * DA_WIDTH + POOL_WIDTH + GQA_WIDTH + 2 * GQA_KV_WIDTH
D_FF = 5632
CONV_W = 3

OFF_DAQ = 0
OFF_DAK = OFF_DAQ + DA_WIDTH
OFF_DAV = OFF_DAK + DA_WIDTH
OFF_POOL = OFF_DAV + DA_WIDTH
OFF_GQ = OFF_POOL + POOL_WIDTH
OFF_GK = OFF_GQ + GQA_WIDTH
OFF_GV = OFF_GK + GQA_KV_WIDTH

V7X_VMEM_LIMIT_BYTES = 56 * 1024 * 1024
HALO = 16
FFN_TF = 512
PROJ_SUB = 256
CAST_STEPS = 16


def _cparams(*sem):
    return pltpu.CompilerParams(dimension_semantics=sem, vmem_limit_bytes=V7X_VMEM_LIMIT_BYTES)


def _dot(a, b):
    return jnp.dot(a, b, preferred_element_type=F32)


def _dot_nt(a, b):
    return lax.dot_general(a, b, (((1,), (1,)), ((), ())), preferred_element_type=F32)


def _rms(x, gain):
    return x * lax.rsqrt(jnp.mean(x * x, axis=-1, keepdims=True) + EPS) * gain


def _modulate(x, gain, shift, scale):
    return _rms(x, gain) * (1.0 + scale) + shift


def _silu(x):
    return x / (1.0 + jnp.exp2(x * -LOG2E))


def _cast_specs(jobs, step_of, grid_steps):
    in_specs, out_specs, out_shapes, args = [], [], [], []
    n = min(CAST_STEPS, 1 << (grid_steps.bit_length() - 1))
    for w, layer in jobs:
        _, r, c = w.shape
        rb = r // n
        assert rb * n == r and rb % HALO == 0
        blk = lambda *g: jnp.minimum(step_of(*g), n - 1)
        in_specs.append(pl.BlockSpec((None, rb, c), lambda *g, layer=layer, blk=blk: (layer, blk(*g), 0)))
        out_specs.append(pl.BlockSpec((rb, c), lambda *g, blk=blk: (blk(*g), 0)))
        out_shapes.append(jax.ShapeDtypeStruct((r, c), BF16))
        args.append(w)
    return in_specs, out_specs, out_shapes, args


def _run_casts(in_refs, out_refs):
    for src, dst in zip(in_refs, out_refs):
        dst[...] = src[...].astype(BF16)


def _mod_kernel(*refs, n_cast):
    c_ref, w_ref, b_ref = refs[:3]
    cast_in = refs[3:3 + n_cast]
    o_ref = refs[3 + n_cast]
    cast_out = refs[4 + n_cast:]
    s = _silu(c_ref[...]).astype(BF16)
    o_ref[0] = _dot(s, w_ref[0].astype(BF16)) + b_ref[0]
    _run_casts(cast_in, cast_out)


def _mod_call(c8, w_mod, b_mod, cast_jobs, tn=1024):
    depth, d, n = w_mod.shape
    nj = n // tn
    c_in, c_out, c_shapes, c_args = _cast_specs(cast_jobs, lambda l, j: l * nj + j, depth * nj)
    return pl.pallas_call(
        functools.partial(_mod_kernel, n_cast=len(cast_jobs)),
        grid=(depth, nj),
        in_specs=[pl.BlockSpec((8, d), lambda l, j: (0, 0)),
                  pl.BlockSpec((1, d, tn), lambda l, j: (l, 0, j)),
                  pl.BlockSpec((1, 1, tn), lambda l, j: (l, 0, j))] + c_in,
        out_specs=[pl.BlockSpec((1, 8, tn), lambda l, j: (l, 0, j))] + c_out,
        out_shape=[jax.ShapeDtypeStruct((depth, 8, n), F32)] + c_shapes,
        compiler_params=_cparams("arbitrary", "arbitrary"),
        name="adaln_mod",
    )(c8, w_mod, b_mod.reshape(depth, 1, n), *c_args)


def _rope(x, c, s1, s2, shift):
    return x * c + pltpu.roll(x, HEAD_DIM - shift, 1) * s1 + pltpu.roll(x, shift, 1) * s2


def _inproj_kernel(*refs, use_rope, sub):
    x_ref, sh_ref, sc_ref, nrm_ref, w_ref, qn_ref, kn_ref = refs[:7]
    pos = 7
    rope_refs = ()
    if use_rope:
        rope_refs = refs[pos:pos + 6]
        pos += 6
    daq_ref, dak_ref, dav_ref, pool_ref, gq_ref, gk_ref, gv_ref = refs[pos:]

    def head_cols(i):
        return slice(i * HEAD_DIM, (i + 1) * HEAD_DIM)

    for r in range(x_ref.shape[0] // sub):
        rows = slice(r * sub, (r + 1) * sub)
        h = _modulate(x_ref[rows, :], nrm_ref[...], sh_ref[0], sc_ref[0]).astype(BF16)
        dac, das1, das2, gc, gs1, gs2 = (t[rows, :] for t in rope_refs) if use_rope else (None,) * 6

        def proj(off, width):
            return _dot(h, w_ref[:, off:off + width])

        a = proj(OFF_DAQ, DA_WIDTH)
        for i in range(DA_HEADS):
            t = a[:, head_cols(i)]
            if use_rope:
                t = _rope(t, dac, das1, das2, DA_HALF_DIM // 4)
            daq_ref[rows, head_cols(i)] = (t * (DA_HALF_DIM ** -0.5 * LOG2E)).astype(BF16)
        a = proj(OFF_DAK, DA_WIDTH)
        for i in range(DA_HEADS):
            t = a[:, head_cols(i)]
            if use_rope:
                t = _rope(t, dac, das1, das2, DA_HALF_DIM // 4)
            dak_ref[rows, head_cols(i)] = t.astype(BF16)
        dav_ref[rows, :] = proj(OFF_DAV, DA_WIDTH).astype(BF16)
        pool_ref[rows, :] = proj(OFF_POOL, POOL_WIDTH)
        a = proj(OFF_GQ, GQA_WIDTH)
        for i in range(GQA_Q_HEADS):
            t = _rms(a[:, head_cols(i)], qn_ref[...])
            if use_rope:
                t = _rope(t, gc, gs1, gs2, HEAD_DIM // 4)
            gq_ref[rows, head_cols(i)] = (t * (HEAD_DIM ** -0.5 * LOG2E)).astype(BF16)
        a = proj(OFF_GK, GQA_KV_WIDTH)
        for i in range(GQA_KV_HEADS):
            t = _rms(a[:, head_cols(i)], kn_ref[...])
            if use_rope:
                t = _rope(t, gc, gs1, gs2, HEAD_DIM // 4)
            gk_ref[rows, head_cols(i)] = t.astype(BF16)
        gv_ref[rows, :] = proj(OFF_GV, GQA_KV_WIDTH).astype(BF16)


def _inproj_call(x, shift, scale, norm_gain, w_in_bf, q_norm, k_norm, rope, tm, rows_per_mod):
    m, d = x.shape
    tpm = rows_per_mod // tm
    row = lambda i: (i, 0)
    const = lambda i: (0, 0)
    in_specs = [pl.BlockSpec((tm, d), row),
                pl.BlockSpec((1, 1, d), lambda i: (i // tpm, 0, 0)),
                pl.BlockSpec((1, 1, d), lambda i: (i // tpm, 0, 0)),
                pl.BlockSpec((1, d), const),
                pl.BlockSpec((d, IN_COLS), const, pipeline_mode=pl.Buffered(1)),
                pl.BlockSpec((1, HEAD_DIM), const),
                pl.BlockSpec((1, HEAD_DIM), const)]
    args = [x, shift, scale, norm_gain.reshape(1, d), w_in_bf,
            q_norm.reshape(1, HEAD_DIM), k_norm.reshape(1, HEAD_DIM)]
    if rope is not None:
        tps = rope[0].shape[0] // tm
        in_specs += [pl.BlockSpec((tm, HEAD_DIM), lambda i: (i % tps, 0))] * 6
        args += list(rope)
    widths = (DA_WIDTH, DA_WIDTH, DA_WIDTH, POOL_WIDTH, GQA_WIDTH, GQA_KV_WIDTH, GQA_KV_WIDTH)
    dtypes = (BF16, BF16, BF16, F32, BF16, BF16, BF16)
    return pl.pallas_call(
        functools.partial(_inproj_kernel, use_rope=rope is not None, sub=min(tm, PROJ_SUB)),
        grid=(m // tm,),
        in_specs=in_specs,
        out_specs=[pl.BlockSpec((tm, w), row) for w in widths],
        out_shape=[jax.ShapeDtypeStruct((m, w), dt) for w, dt in zip(widths, dtypes)],
        compiler_params=_cparams("parallel"),
        name="in_proj",
    )(*args)


def _attn_kernel(*refs, nseg, tq, tiles_per_iter, lambda_init, n_cast):
    da = lambda_init is not None
    q_ref = refs[0]
    k_refs = refs[1:1 + nseg]
    v_refs = refs[1 + nseg:1 + 2 * nseg]
    pos = 1 + 2 * nseg
    if da:
        lq1, lk1, lq2, lk2, subln_ref = refs[pos:pos + 5]
        pos += 5
    cast_in = refs[pos:pos + n_cast]
    o_ref = refs[pos + n_cast]
    cast_out = refs[pos + n_cast + 1:pos + 2 * n_cast + 1]
    kcat, vcat = refs[pos + 2 * n_cast + 1:]
    _run_casts(cast_in, cast_out)

    off = 0
    for kr, vr in zip(k_refs, v_refs):
        n = kr.shape[0]
        kcat[off:off + n, :] = kr[...]
        vcat[off:off + n, :HEAD_DIM] = vr[...]
        off += n
    vcat[:, HEAD_DIM:] = jnp.ones((off, HEAD_DIM), BF16)

    def attend(q):
        s = _dot_nt(q, kcat[...])
        e = jnp.exp2(s - jnp.max(s, axis=-1, keepdims=True)).astype(BF16)
        o = _dot(e, vcat[...])
        return o[:, :HEAD_DIM] / o[:, HEAD_DIM:]

    if da:
        lam = (jnp.exp(jnp.sum(lq1[...] * lk1[...], axis=-1, keepdims=True))
               - jnp.exp(jnp.sum(lq2[...] * lk2[...], axis=-1, keepdims=True)) + lambda_init)
        subln = subln_ref[...]

    def tile(t):
        rows = pl.ds(pl.multiple_of(t * tq, tq), tq)
        for g in range(q_ref.shape[1] // HEAD_DIM):
            cols = slice(g * HEAD_DIM, (g + 1) * HEAD_DIM)
            q = q_ref[rows, cols]
            if da:
                lane = lax.broadcasted_iota(jnp.int32, q.shape, 1)
                zero = jnp.zeros_like(q)
                a1 = attend(jnp.where(lane < DA_HALF_DIM, q, zero))
                a2 = attend(jnp.where(lane >= DA_HALF_DIM, q, zero))
                o = _rms(a1 - lam * a2, subln) * (1.0 - lambda_init)
            else:
                o = attend(q)
            o_ref[rows, cols] = o.astype(o_ref.dtype)

    n_iter = q_ref.shape[0] // (tq * tiles_per_iter)

    def body(it, carry):
        for u in range(tiles_per_iter):
            tile(it * tiles_per_iter + u)
        return carry

    lax.fori_loop(0, n_iter, body, 0)


def _attn_call(q, k_segs, v_segs, seg_lens, sq, tq, tiles_per_iter, n_heads, q_per_kv, out_dtype,
               da_params=None, cast_jobs=()):
    mq = q.shape[0]
    nb = mq // sq
    nseg = len(k_segs)
    sk = sum(seg_lens)
    n_kv = n_heads // q_per_kv
    c_in, c_out, c_shapes, c_args = _cast_specs(cast_jobs, lambda b, h: b * n_kv + h, nb * n_kv)
    qspec = pl.BlockSpec((sq, q_per_kv * HEAD_DIM), lambda b, h: (b, h))
    kv_specs = [pl.BlockSpec((n, HEAD_DIM), lambda b, h: (b, h)) for n in seg_lens]
    in_specs = [qspec] + kv_specs + kv_specs
    args = [q, *k_segs, *v_segs]
    lambda_init = None
    if da_params is not None:
        lam_vecs, subln, lambda_init = da_params
        vec = lambda n: pl.BlockSpec((1, n), lambda b, h: (0, 0))
        in_specs += [vec(DA_HALF_DIM)] * 4 + [vec(HEAD_DIM)]
        args += [v.reshape(1, DA_HALF_DIM) for v in lam_vecs] + [subln.reshape(1, HEAD_DIM)]
    return pl.pallas_call(
        functools.partial(_attn_kernel, nseg=nseg, tq=tq, tiles_per_iter=tiles_per_iter,
                          lambda_init=lambda_init, n_cast=len(cast_jobs)),
        grid=(nb, n_kv),
        in_specs=in_specs + c_in,
        out_specs=[qspec] + c_out,
        out_shape=[jax.ShapeDtypeStruct((mq, n_heads * HEAD_DIM), out_dtype)] + c_shapes,
        scratch_shapes=[pltpu.VMEM((sk, HEAD_DIM), BF16), pltpu.VMEM((sk, 2 * HEAD_DIM), BF16)],
        compiler_params=_cparams("arbitrary", "arbitrary"),
        name="da_attn" if da_params is not None else "gqa_attn",
    )(*args, *c_args)


def _pool_kernel(u_ref, w_ref, ps_ref, o_ref):
    t = u_ref.shape[0]
    row = lax.broadcasted_iota(jnp.int32, (t, HEAD_DIM), 0)
    for g, w in enumerate(POOL_WINDOWS):
        sl = slice(g * HEAD_DIM, (g + 1) * HEAD_DIM)
        u = u_ref[:, sl]
        acc = jnp.zeros_like(u)
        for d in range(-(w // 2), w - w // 2):
            if d == 0:
                acc = acc + u
                continue
            shifted = pltpu.roll(u, (-d) % t, 0)
            valid = (row >= -d) if d < 0 else (row < t - d)
            acc = acc + jnp.where(valid, shifted, 0.0)
        cnt = jnp.minimum(row - w // 2 + w, t) - jnp.maximum(row - w // 2, 0)
        pooled = acc / cnt.astype(F32) - u
        mixed = _dot(pooled.astype(BF16), w_ref[g].astype(BF16))
        o_ref[:, sl] = (mixed * ps_ref[:, sl]).astype(BF16)


def _pool_call(u, w_pool, pool_scale, seq):
    m = u.shape[0]
    return pl.pallas_call(
        _pool_kernel,
        grid=(m // seq,),
        in_specs=[pl.BlockSpec((seq, POOL_WIDTH), lambda b: (b, 0)),
                  pl.BlockSpec(w_pool.shape, lambda b: (0, 0, 0)),
                  pl.BlockSpec((1, POOL_WIDTH), lambda b: (0, 0))],
        out_specs=pl.BlockSpec((seq, POOL_WIDTH), lambda b: (b, 0)),
        out_shape=jax.ShapeDtypeStruct((m, POOL_WIDTH), BF16),
        compiler_params=_cparams("parallel"),
        name="pool_mixer",
    )(u, w_pool, pool_scale.reshape(1, POOL_WIDTH))


def _outproj_kernel(oda_ref, opool_ref, ogqa_ref, gn_ref, w_ref, x_ref, g1_ref,
                    nrm2_ref, sh2_ref, sc2_ref, xo_ref, h2_ref, *, sub):
    for r in range(x_ref.shape[0] // sub):
        rows = slice(r * sub, (r + 1) * sub)
        ogn = _rms(ogqa_ref[rows, :], gn_ref[...]).astype(BF16)
        cat = jnp.concatenate([oda_ref[rows, :], opool_ref[rows, :], ogn], axis=1)
        xn = x_ref[rows, :] + g1_ref[0] * _dot(cat, w_ref[...])
        xo_ref[rows, :] = xn
        h2_ref[rows, :] = _modulate(xn, nrm2_ref[...], sh2_ref[0], sc2_ref[0]).astype(BF16)


def _outproj_call(o_da, o_pool, o_gqa, gqa_out_norm, w_out_bf, x, g1, norm_ffn, sh2, sc2,
                  tm, rows_per_mod):
    m, d = x.shape
    tpm = rows_per_mod // tm
    row = lambda i: (i, 0)
    const = lambda i: (0, 0)
    modrow = pl.BlockSpec((1, 1, d), lambda i: (i // tpm, 0, 0))
    return pl.pallas_call(
        functools.partial(_outproj_kernel, sub=min(tm, PROJ_SUB)),
        grid=(m // tm,),
        in_specs=[pl.BlockSpec((tm, DA_WIDTH), row),
                  pl.BlockSpec((tm, POOL_WIDTH), row),
                  pl.BlockSpec((tm, GQA_WIDTH), row),
                  pl.BlockSpec((1, GQA_WIDTH), const),
                  pl.BlockSpec((d, d), const, pipeline_mode=pl.Buffered(1)),
                  pl.BlockSpec((tm, d), row),
                  modrow,
                  pl.BlockSpec((1, d), const),
                  modrow, modrow],
        out_specs=[pl.BlockSpec((tm, d), row), pl.BlockSpec((tm, d), row)],
        out_shape=[jax.ShapeDtypeStruct((m, d), F32), jax.ShapeDtypeStruct((m, d), BF16)],
        compiler_params=_cparams("parallel"),
        name="out_proj",
    )(o_da, o_pool, o_gqa, gqa_out_norm.reshape(1, GQA_WIDTH), w_out_bf, x, g1,
      norm_ffn.reshape(1, d), sh2, sc2)


def _ffn_kernel(*refs, tm, tiles_per_seq, seq_starts, final):
    (h_ref, hp_ref, hn_ref, wg_ref, wv_ref, cwg_ref, cwv_ref, cbg_ref, cbv_ref, wd_ref,
     x_ref, g2_ref) = refs[:12]
    pos = 12
    if final:
        fn_ref = refs[pos]
        pos += 1
    o_ref, lhs_ref, uga_ref, ugb_ref, uva_ref, uvb_ref = refs[pos:]
    i = pl.program_id(0)
    j = pl.program_id(1)

    @pl.when(j == 0)
    def _():
        first = (i % tiles_per_seq) == 0
        last = (i % tiles_per_seq) == tiles_per_seq - 1
        lhs_ref[0:HALO, :] = jnp.where(first, jnp.zeros_like(hp_ref[...]), hp_ref[...])
        lhs_ref[HALO:HALO + tm, :] = h_ref[...]
        lhs_ref[HALO + tm:, :] = jnp.where(last, jnp.zeros_like(hn_ref[...]), hn_ref[...])
        o_ref[...] = jnp.zeros_like(o_ref)

    lhs = lhs_ref[...]

    def conv(u_ref, cw, cb):
        def taps(start, n, fix=None):
            prev, nxt = u_ref[pl.ds(HALO + start - 1, n), :], u_ref[pl.ds(HALO + start + 1, n), :]
            if fix is not None:
                row = lax.broadcasted_iota(jnp.int32, prev.shape, 0) + start
                prev = jnp.where(row == fix, 0.0, prev)
                nxt = jnp.where(row == fix - 1, 0.0, nxt)
            return prev * cw[0:1] + u_ref[pl.ds(HALO + start, n), :] * cw[1:2] + nxt * cw[2:3] + cb

        c = taps(0, tm)
        for b in seq_starts:
            c = jnp.concatenate([c[:b - 8], taps(b - 8, 16, fix=b), c[b + 8:]], axis=0)
        return c

    tf = wg_ref.shape[1]
    hf = tf // 2
    uga_ref[...] = _dot(lhs, wg_ref[:, :hf])
    ugb_ref[...] = _dot(lhs, wg_ref[:, hf:])
    uva_ref[...] = _dot(lhs, wv_ref[:, :hf])
    uvb_ref[...] = _dot(lhs, wv_ref[:, hf:])
    gate_a = _silu(conv(uga_ref, cwg_ref[:, :hf], cbg_ref[:, :hf]))
    gate_b = _silu(conv(ugb_ref, cwg_ref[:, hf:], cbg_ref[:, hf:]))
    act_a = (gate_a * conv(uva_ref, cwv_ref[:, :hf], cbv_ref[:, :hf])).astype(BF16)
    act_b = (gate_b * conv(uvb_ref, cwv_ref[:, hf:], cbv_ref[:, hf:])).astype(BF16)
    o_ref[...] += _dot(act_a, wd_ref[:hf, :]) + _dot(act_b, wd_ref[hf:, :])

    @pl.when(j == pl.num_programs(1) - 1)
    def _():
        out = x_ref[...] + g2_ref[0] * o_ref[...]
        if final:
            out = _rms(out, fn_ref[...])
        o_ref[...] = out


def _ffn_call(h2, w_up_bf, conv_w, conv_b, w_down_bf, x, g2, final_norm, tm, seq, rows_per_mod):
    m, d = x.shape
    tf = FFN_TF
    nj = D_FF // tf
    assert seq % tm == 0 or tm % seq == 0
    tps = max(seq // tm, 1)
    seq_starts = tuple(range(seq, tm, seq))
    tpm = rows_per_mod // tm
    hb = tm // HALO
    nhb = m // HALO
    final = final_norm is not None
    row = lambda i, j: (i, 0)
    gate_col = lambda i, j: (0, j)
    val_col = lambda i, j: (0, j + nj)
    in_specs = [pl.BlockSpec((tm, d), row),
                pl.BlockSpec((HALO, d), lambda i, j: (jnp.maximum(i * hb - 1, 0), 0)),
                pl.BlockSpec((HALO, d), lambda i, j: (jnp.minimum((i + 1) * hb, nhb - 1), 0)),
                pl.BlockSpec((d, tf), gate_col),
                pl.BlockSpec((d, tf), val_col),
                pl.BlockSpec((CONV_W, tf), gate_col),
                pl.BlockSpec((CONV_W, tf), val_col),
                pl.BlockSpec((1, tf), gate_col),
                pl.BlockSpec((1, tf), val_col),
                pl.BlockSpec((tf, d), lambda i, j: (j, 0)),
                pl.BlockSpec((tm, d), row),
                pl.BlockSpec((1, 1, d), lambda i, j: (i // tpm, 0, 0))]
    cb = conv_b.reshape(1, 2 * D_FF)
    args = [h2, h2, h2, w_up_bf, w_up_bf, conv_w, conv_w, cb, cb, w_down_bf, x, g2]
    if final:
        in_specs.append(pl.BlockSpec((1, d), lambda i, j: (0, 0)))
        args.append(final_norm.reshape(1, d))
    return pl.pallas_call(
        functools.partial(_ffn_kernel, tm=tm, tiles_per_seq=tps, seq_starts=seq_starts, final=final),
        grid=(m // tm, nj),
        in_specs=in_specs,
        out_specs=pl.BlockSpec((tm, d), row),
        out_shape=jax.ShapeDtypeStruct((m, d), F32),
        scratch_shapes=([pltpu.VMEM((tm + 2 * HALO, d), BF16)]
                        + [pltpu.VMEM((tm + 2 * HALO, tf // 2), F32)] * 4),
        compiler_params=_cparams("parallel", "arbitrary"),
        name="conv_ffn",
    )(*args)


def _rope_tables(seq, rot_dim):
    rows = seq // GRID_W
    row = jnp.repeat(jnp.arange(rows, dtype=F32), GRID_W)
    col = jnp.tile(jnp.arange(GRID_W, dtype=F32), rows)
    axis_dim = rot_dim // 2
    nf = axis_dim // 2
    freqs = ROPE_BASE ** (-(jnp.arange(nf, dtype=F32) * 2.0 / axis_dim))
    ar, ac = row[:, None] * freqs, col[:, None] * freqs
    cos = jnp.concatenate([jnp.cos(ar)] * 2 + [jnp.cos(ac)] * 2, axis=-1)
    sin = jnp.concatenate([jnp.sin(ar)] * 2 + [jnp.sin(ac)] * 2, axis=-1)
    first = jnp.tile(jnp.concatenate([jnp.ones((nf,), F32), jnp.zeros((nf,), F32)]), 2)
    reps = HEAD_DIM // rot_dim
    return (jnp.tile(cos, (1, reps)), jnp.tile(-sin * first, (1, reps)),
            jnp.tile(sin * (1.0 - first), (1, reps)))


def kernel(x, c, ctx, c_ctx, w_mod, b_mod, norm_mix, norm_ffn, w_in, da_lambda_q1, da_lambda_k1,
           da_lambda_q2, da_lambda_k2, da_subln, gqa_q_norm, gqa_k_norm, pool_w, pool_scale,
           gqa_out_norm, w_out, w_up, conv_w, conv_b, w_down, final_norm):
    nb, seq, d = x.shape
    n_ctx = ctx.shape[1]
    depth = w_mod.shape[0]
    assert d == D_MODEL and seq % GRID_W == 0 and nb <= 7

    tm_lat = min(512, seq)
    tm_ctx = min(512, nb * n_ctx)
    tq_lat = min(256, seq)
    tq_ctx = min(256, n_ctx)
    tpi_lat = math.gcd(seq // tq_lat, 8)

    c8 = jnp.zeros((8, d), F32).at[:nb].set(c).at[nb].set(c_ctx)
    mods, w_in_bf = _mod_call(c8, w_mod, b_mod, [(w_in, 0)])

    rope = _rope_tables(seq, DA_HALF_DIM) + _rope_tables(seq, HEAD_DIM)

    xl = x.reshape(nb * seq, d)
    xc = ctx.reshape(nb * n_ctx, d)

    for l in range(depth):
        last = l == depth - 1
        lambda_init = 0.8 - 0.6 * math.exp(-0.3 * l)
        m = mods[l]
        lat = [m[:nb, k * d:(k + 1) * d].reshape(nb, 1, d) for k in range(N_MOD)]
        cm = [m[nb:nb + 1, k * d:(k + 1) * d].reshape(1, 1, d) for k in range(N_MOD)]
        da_params = ((da_lambda_q1[l], da_lambda_k1[l], da_lambda_q2[l], da_lambda_k2[l]),
                     da_subln[l], lambda_init)

        pl_ = _inproj_call(xl, lat[0], lat[1], norm_mix[l], w_in_bf, gqa_q_norm[l], gqa_k_norm[l],
                           rope, tm_lat, seq)
        pc_ = _inproj_call(xc, cm[0], cm[1], norm_mix[l], w_in_bf, gqa_q_norm[l], gqa_k_norm[l],
                           None, tm_ctx, nb * n_ctx)
        daq_l, dak_l, dav_l, pool_l, gq_l, gk_l, gv_l = pl_
        daq_c, dak_c, dav_c, pool_c, gq_c, gk_c, gv_c = pc_

        o_da, w_up_bf, w_down_bf = _attn_call(
            daq_l, [dak_c, dak_l], [dav_c, dav_l], [n_ctx, seq], seq, tq_lat, tpi_lat,
            DA_HEADS, 1, BF16, da_params, cast_jobs=[(w_up, l), (w_down, l)])
        o_gqa, w_out_bf, *w_in_next = _attn_call(
            gq_l, [gk_c, gk_l], [gv_c, gv_l], [n_ctx, seq], seq, tq_lat, tpi_lat,
            GQA_Q_HEADS, GQA_GROUP, F32, cast_jobs=[(w_out, l)] + ([] if last else [(w_in, l + 1)]))
        ffn_w = (w_up_bf, conv_w[l], conv_b[l], w_down_bf)
        o_pool = _pool_call(pool_l, pool_w[l], pool_scale[l], seq)
        xl, h2 = _outproj_call(o_da, o_pool, o_gqa, gqa_out_norm[l], w_out_bf, xl, lat[2],
                               norm_ffn[l], lat[3], lat[4], tm_lat, seq)
        xl = _ffn_call(h2, *ffn_w, xl, lat[5], final_norm if last else None, tm_lat, seq, seq)
        if not last:
            o_da, = _attn_call(daq_c, [dak_c], [dav_c], [n_ctx], n_ctx, tq_ctx, 1,
                               DA_HEADS, 1, BF16, da_params)
            o_gqa, = _attn_call(gq_c, [gk_c], [gv_c], [n_ctx], n_ctx, tq_ctx, 1,
                                GQA_Q_HEADS, GQA_GROUP, F32)
            o_pool = _pool_call(pool_c, pool_w[l], pool_scale[l], n_ctx)
            xc, h2 = _outproj_call(o_da, o_pool, o_gqa, gqa_out_norm[l], w_out_bf, xc, cm[2],
                                   norm_ffn[l], cm[3], cm[4], tm_ctx, nb * n_ctx)
            xc = _ffn_call(h2, *ffn_w, xc, cm[5], None, tm_ctx, n_ctx, nb * n_ctx)
            w_in_bf, = w_in_next

    return xl.reshape(nb, seq, d)
```

```python
import functools
import math

import jax
import jax.numpy as jnp
import numpy as np
from jax import lax
from jax.experimental import pallas as pl
from jax.experimental.pallas import tpu as pltpu

F32 = jnp.float32
BF16 = jnp.bfloat16

D_MODEL = 2048
GRID_W = 64
EPS = 1e-6
ROPE_BASE = 10000.0
N_MOD = 6
LOG2E = math.log2(math.e)

DA_HEADS = 6
DA_HALF_DIM = 64
HEAD_DIM = 128
DA_WIDTH = DA_HEADS * HEAD_DIM
POOL_WINDOWS = (2, 4, 8, 16)
POOL_WIDTH = len(POOL_WINDOWS) * HEAD_DIM
GQA_Q_HEADS = 6
GQA_KV_HEADS = 2
GQA_GROUP = GQA_Q_HEADS // GQA_KV_HEADS
GQA_WIDTH = GQA_Q_HEADS * HEAD_DIM
GQA_KV_WIDTH = GQA_KV_HEADS * HEAD_DIM
IN_COLS = 3 * DA_WIDTH + POOL_WIDTH + GQA_WIDTH + 2 * GQA_KV_WIDTH
D_FF = 5632
CONV_W = 3

OFF_DAQ = 0
OFF_DAK = OFF_DAQ + DA_WIDTH
OFF_DAV = OFF_DAK + DA_WIDTH
OFF_POOL = OFF_DAV + DA_WIDTH
OFF_GQ = OFF_POOL + POOL_WIDTH
OFF_GK = OFF_GQ + GQA_WIDTH
OFF_GV = OFF_GK + GQA_KV_WIDTH

V7X_VMEM_LIMIT_BYTES = 56 * 1024 * 1024
HALO = 16
FFN_TF = 512
PROJ_SUB = 256
CAST_STEPS = 16
POOL_PAD = 16


def _cparams(*sem):
    return pltpu.CompilerParams(dimension_semantics=sem, vmem_limit_bytes=V7X_VMEM_LIMIT_BYTES)


def _dot(a, b):
    return jnp.dot(a, b, preferred_element_type=F32)


def _dot_nt(a, b):
    return lax.dot_general(a, b, (((1,), (1,)), ((), ())), preferred_element_type=F32)


def _rms(x, gain):
    return x * lax.rsqrt(jnp.mean(x * x, axis=-1, keepdims=True) + EPS) * gain


def _modulate(x, gain, shift, scale):
    return _rms(x, gain) * (1.0 + scale) + shift


def _silu(x):
    return x / (1.0 + jnp.exp2(x * -LOG2E))


def _cast_specs(jobs, step_of, grid_steps):
    in_specs, out_specs, out_shapes, args = [], [], [], []
    n = min(CAST_STEPS, 1 << (grid_steps.bit_length() - 1))
    for w, layer in jobs:
        _, r, c = w.shape
        rb = r // n
        assert rb * n == r and rb % HALO == 0
        blk = lambda *g: jnp.minimum(step_of(*g), n - 1)
        in_specs.append(pl.BlockSpec((None, rb, c), lambda *g, layer=layer, blk=blk: (layer, blk(*g), 0)))
        out_specs.append(pl.BlockSpec((rb, c), lambda *g, blk=blk: (blk(*g), 0)))
        out_shapes.append(jax.ShapeDtypeStruct((r, c), BF16))
        args.append(w)
    return in_specs, out_specs, out_shapes, args


def _run_casts(in_refs, out_refs):
    for src, dst in zip(in_refs, out_refs):
        dst[...] = src[...].astype(BF16)


def _mod_kernel(*refs, n_cast):
    c_ref, w_ref, b_ref = refs[:3]
    cast_in = refs[3:3 + n_cast]
    o_ref = refs[3 + n_cast]
    cast_out = refs[4 + n_cast:]
    s = _silu(c_ref[...]).astype(BF16)
    o_ref[0] = _dot(s, w_ref[0].astype(BF16)) + b_ref[0]
    _run_casts(cast_in, cast_out)


def _mod_call(c8, w_mod, b_mod, cast_jobs, tn=1024):
    depth, d, n = w_mod.shape
    nj = n // tn
    c_in, c_out, c_shapes, c_args = _cast_specs(cast_jobs, lambda l, j: l * nj + j, depth * nj)
    return pl.pallas_call(
        functools.partial(_mod_kernel, n_cast=len(cast_jobs)),
        grid=(depth, nj),
        in_specs=[pl.BlockSpec((8, d), lambda l, j: (0, 0)),
                  pl.BlockSpec((1, d, tn), lambda l, j: (l, 0, j)),
                  pl.BlockSpec((1, 1, tn), lambda l, j: (l, 0, j))] + c_in,
        out_specs=[pl.BlockSpec((1, 8, tn), lambda l, j: (l, 0, j))] + c_out,
        out_shape=[jax.ShapeDtypeStruct((depth, 8, n), F32)] + c_shapes,
        compiler_params=_cparams("arbitrary", "arbitrary"),
        name="adaln_mod",
    )(c8, w_mod, b_mod.reshape(depth, 1, n), *c_args)


def _rope(x, c, s1, s2, shift):
    return x * c + pltpu.roll(x, HEAD_DIM - shift, 1) * s1 + pltpu.roll(x, shift, 1) * s2


def _inproj_kernel(*refs, use_rope, sub):
    x_ref, sh_ref, sc_ref, nrm_ref, w_ref, qn_ref, kn_ref = refs[:7]
    pos = 7
    rope_refs = ()
    if use_rope:
        rope_refs = refs[pos:pos + 6]
        pos += 6
    daq_ref, dak_ref, dav_ref, pool_ref, gq_ref, gk_ref, gv_ref = refs[pos:]

    def head_cols(i):
        return slice(i * HEAD_DIM, (i + 1) * HEAD_DIM)

    for r in range(x_ref.shape[0] // sub):
        rows = slice(r * sub, (r + 1) * sub)
        h = _modulate(x_ref[rows, :], nrm_ref[...], sh_ref[0], sc_ref[0]).astype(BF16)
        dac, das1, das2, gc, gs1, gs2 = (t[rows, :] for t in rope_refs) if use_rope else (None,) * 6

        def proj(off, width):
            return _dot(h, w_ref[:, off:off + width])

        a = proj(OFF_DAQ, DA_WIDTH)
        for i in range(DA_HEADS):
            t = a[:, head_cols(i)]
            if use_rope:
                t = _rope(t, dac, das1, das2, DA_HALF_DIM // 4)
            daq_ref[rows, head_cols(i)] = (t * (DA_HALF_DIM ** -0.5 * LOG2E)).astype(BF16)
        a = proj(OFF_DAK, DA_WIDTH)
        for i in range(DA_HEADS):
            t = a[:, head_cols(i)]
            if use_rope:
                t = _rope(t, dac, das1, das2, DA_HALF_DIM // 4)
            dak_ref[rows, head_cols(i)] = t.astype(BF16)
        dav_ref[rows, :] = proj(OFF_DAV, DA_WIDTH).astype(BF16)
        pool_ref[rows, :] = proj(OFF_POOL, POOL_WIDTH)
        a = proj(OFF_GQ, GQA_WIDTH)
        for i in range(GQA_Q_HEADS):
            t = _rms(a[:, head_cols(i)], qn_ref[...])
            if use_rope:
                t = _rope(t, gc, gs1, gs2, HEAD_DIM // 4)
            gq_ref[rows, head_cols(i)] = (t * (HEAD_DIM ** -0.5 * LOG2E)).astype(BF16)
        a = proj(OFF_GK, GQA_KV_WIDTH)
        for i in range(GQA_KV_HEADS):
            t = _rms(a[:, head_cols(i)], kn_ref[...])
            if use_rope:
                t = _rope(t, gc, gs1, gs2, HEAD_DIM // 4)
            gk_ref[rows, head_cols(i)] = t.astype(BF16)
        gv_ref[rows, :] = proj(OFF_GV, GQA_KV_WIDTH).astype(BF16)


def _mod_spec(ms, chunk, tpm):
    _, layer, fixed_row = ms

    def index(i, *_):
        r = i // tpm if fixed_row is None else fixed_row
        return ((layer * 8 + r) * N_MOD + chunk, 0, 0)

    return pl.BlockSpec((1, 1, D_MODEL), index)


def _inproj_call(x, ms, norm_gain, w_in_bf, q_norm, k_norm, rope, tm, rows_per_mod):
    m, d = x.shape
    tpm = rows_per_mod // tm
    row = lambda i: (i, 0)
    const = lambda i: (0, 0)
    shift, scale = ms[0], ms[0]
    in_specs = [pl.BlockSpec((tm, d), row),
                _mod_spec(ms, 0, tpm),
                _mod_spec(ms, 1, tpm),
                pl.BlockSpec((1, d), const),
                pl.BlockSpec((d, IN_COLS), const, pipeline_mode=pl.Buffered(1)),
                pl.BlockSpec((1, HEAD_DIM), const),
                pl.BlockSpec((1, HEAD_DIM), const)]
    args = [x, shift, scale, norm_gain.reshape(1, d), w_in_bf,
            q_norm.reshape(1, HEAD_DIM), k_norm.reshape(1, HEAD_DIM)]
    if rope is not None:
        tps = rope[0].shape[0] // tm
        in_specs += [pl.BlockSpec((tm, HEAD_DIM), lambda i: (i % tps, 0))] * 6
        args += list(rope)
    widths = (DA_WIDTH, DA_WIDTH, DA_WIDTH, POOL_WIDTH, GQA_WIDTH, GQA_KV_WIDTH, GQA_KV_WIDTH)
    dtypes = (BF16, BF16, BF16, F32, BF16, BF16, BF16)
    return pl.pallas_call(
        functools.partial(_inproj_kernel, use_rope=rope is not None, sub=min(tm, PROJ_SUB)),
        grid=(m // tm,),
        in_specs=in_specs,
        out_specs=[pl.BlockSpec((tm, w), row) for w in widths],
        out_shape=[jax.ShapeDtypeStruct((m, w), dt) for w, dt in zip(widths, dtypes)],
        compiler_params=_cparams("parallel"),
        name="in_proj",
    )(*args)


def _attn_kernel(*refs, nseg, tq, tiles_per_iter, lambda_init, n_cast):
    da = lambda_init is not None
    q_ref = refs[0]
    k_refs = refs[1:1 + nseg]
    v_refs = refs[1 + nseg:1 + 2 * nseg]
    pos = 1 + 2 * nseg
    if da:
        lq1, lk1, lq2, lk2, subln_ref = refs[pos:pos + 5]
        pos += 5
    cast_in = refs[pos:pos + n_cast]
    o_ref = refs[pos + n_cast]
    cast_out = refs[pos + n_cast + 1:pos + 2 * n_cast + 1]
    kcat, vcat = refs[pos + 2 * n_cast + 1:]
    _run_casts(cast_in, cast_out)

    off = 0
    for kr, vr in zip(k_refs, v_refs):
        n = kr.shape[0]
        kcat[off:off + n, :] = kr[...]
        vcat[off:off + n, :HEAD_DIM] = vr[...]
        off += n
    vcat[:, HEAD_DIM:] = jnp.ones((off, HEAD_DIM), BF16)

    def attend(q):
        s = _dot_nt(q, kcat[...])
        e = jnp.exp2(s - jnp.max(s, axis=-1, keepdims=True)).astype(BF16)
        o = _dot(e, vcat[...])
        return o[:, :HEAD_DIM] / o[:, HEAD_DIM:]

    if da:
        lam = (jnp.exp(jnp.sum(lq1[...] * lk1[...], axis=-1, keepdims=True))
               - jnp.exp(jnp.sum(lq2[...] * lk2[...], axis=-1, keepdims=True)) + lambda_init)
        subln = subln_ref[...]

    def tile(t):
        rows = pl.ds(pl.multiple_of(t * tq, tq), tq)
        for g in range(q_ref.shape[1] // HEAD_DIM):
            cols = slice(g * HEAD_DIM, (g + 1) * HEAD_DIM)
            q = q_ref[rows, cols]
            if da:
                lane = lax.broadcasted_iota(jnp.int32, q.shape, 1)
                zero = jnp.zeros_like(q)
                a1 = attend(jnp.where(lane < DA_HALF_DIM, q, zero))
                a2 = attend(jnp.where(lane >= DA_HALF_DIM, q, zero))
                o = _rms(a1 - lam * a2, subln) * (1.0 - lambda_init)
            else:
                o = attend(q)
            o_ref[rows, cols] = o.astype(o_ref.dtype)

    n_iter = q_ref.shape[0] // (tq * tiles_per_iter)

    def body(it, carry):
        for u in range(tiles_per_iter):
            tile(it * tiles_per_iter + u)
        return carry

    lax.fori_loop(0, n_iter, body, 0)


def _attn_call(q, k_segs, v_segs, seg_lens, sq, tq, tiles_per_iter, n_heads, q_per_kv, out_dtype,
               da_params=None, cast_jobs=()):
    mq = q.shape[0]
    nb = mq // sq
    nseg = len(k_segs)
    sk = sum(seg_lens)
    n_kv = n_heads // q_per_kv
    c_in, c_out, c_shapes, c_args = _cast_specs(cast_jobs, lambda b, h: b * n_kv + h, nb * n_kv)
    qspec = pl.BlockSpec((sq, q_per_kv * HEAD_DIM), lambda b, h: (b, h))
    kv_specs = [pl.BlockSpec((n, HEAD_DIM), lambda b, h: (b, h)) for n in seg_lens]
    in_specs = [qspec] + kv_specs + kv_specs
    args = [q, *k_segs, *v_segs]
    lambda_init = None
    if da_params is not None:
        lam_vecs, subln, lambda_init = da_params
        vec = lambda n: pl.BlockSpec((1, n), lambda b, h: (0, 0))
        in_specs += [vec(DA_HALF_DIM)] * 4 + [vec(HEAD_DIM)]
        args += [v.reshape(1, DA_HALF_DIM) for v in lam_vecs] + [subln.reshape(1, HEAD_DIM)]
    return pl.pallas_call(
        functools.partial(_attn_kernel, nseg=nseg, tq=tq, tiles_per_iter=tiles_per_iter,
                          lambda_init=lambda_init, n_cast=len(cast_jobs)),
        grid=(nb, n_kv),
        in_specs=in_specs + c_in,
        out_specs=[qspec] + c_out,
        out_shape=[jax.ShapeDtypeStruct((mq, n_heads * HEAD_DIM), out_dtype)] + c_shapes,
        scratch_shapes=[pltpu.VMEM((sk, HEAD_DIM), BF16), pltpu.VMEM((sk, 2 * HEAD_DIM), BF16)],
        compiler_params=_cparams("arbitrary", "arbitrary"),
        name="da_attn" if da_params is not None else "gqa_attn",
    )(*args, *c_args)


def _pool_kernel(u_ref, w_ref, ps_ref, o_ref):
    t = u_ref.shape[0]
    row = lax.broadcasted_iota(jnp.int32, (t, HEAD_DIM), 0)
    for g, w in enumerate(POOL_WINDOWS):
        sl = slice(g * HEAD_DIM, (g + 1) * HEAD_DIM)
        u = u_ref[:, sl]
        zeros = jnp.zeros((POOL_PAD, HEAD_DIM), F32)
        ue = jnp.concatenate([zeros, u, zeros], axis=0)
        n = t + 2 * POOL_PAD
        acc = ue + pltpu.roll(ue, 1, 0)
        m = 2
        while m < w:
            acc = pltpu.roll(acc, m // 2, 0) + pltpu.roll(acc, n - m // 2, 0)
            m *= 2
        cnt = jnp.minimum(row - w // 2 + w, t) - jnp.maximum(row - w // 2, 0)
        pooled = acc[POOL_PAD:POOL_PAD + t] / cnt.astype(F32) - u
        mixed = _dot(pooled.astype(BF16), w_ref[g].astype(BF16))
        o_ref[:, sl] = (mixed * ps_ref[:, sl]).astype(BF16)


def _pool_call(u, w_pool, pool_scale, seq):
    m = u.shape[0]
    return pl.pallas_call(
        _pool_kernel,
        grid=(m // seq,),
        in_specs=[pl.BlockSpec((seq, POOL_WIDTH), lambda b: (b, 0)),
                  pl.BlockSpec(w_pool.shape, lambda b: (0, 0, 0)),
                  pl.BlockSpec((1, POOL_WIDTH), lambda b: (0, 0))],
        out_specs=pl.BlockSpec((seq, POOL_WIDTH), lambda b: (b, 0)),
        out_shape=jax.ShapeDtypeStruct((m, POOL_WIDTH), BF16),
        compiler_params=_cparams("parallel"),
        name="pool_mixer",
    )(u, w_pool, pool_scale.reshape(1, POOL_WIDTH))


def _outproj_kernel(oda_ref, opool_ref, ogqa_ref, gn_ref, w_ref, x_ref, g1_ref,
                    nrm2_ref, sh2_ref, sc2_ref, xo_ref, h2_ref, *, sub):
    for r in range(x_ref.shape[0] // sub):
        rows = slice(r * sub, (r + 1) * sub)
        ogn = _rms(ogqa_ref[rows, :], gn_ref[...]).astype(BF16)
        cat = jnp.concatenate([oda_ref[rows, :], opool_ref[rows, :], ogn], axis=1)
        xn = x_ref[rows, :] + g1_ref[0] * _dot(cat, w_ref[...])
        xo_ref[rows, :] = xn
        h2_ref[rows, :] = _modulate(xn, nrm2_ref[...], sh2_ref[0], sc2_ref[0]).astype(BF16)


def _outproj_call(o_da, o_pool, o_gqa, gqa_out_norm, w_out_bf, x, ms, norm_ffn, tm, rows_per_mod):
    m, d = x.shape
    tpm = rows_per_mod // tm
    row = lambda i: (i, 0)
    const = lambda i: (0, 0)
    g1 = sh2 = sc2 = ms[0]
    return pl.pallas_call(
        functools.partial(_outproj_kernel, sub=min(tm, PROJ_SUB)),
        grid=(m // tm,),
        in_specs=[pl.BlockSpec((tm, DA_WIDTH), row),
                  pl.BlockSpec((tm, POOL_WIDTH), row),
                  pl.BlockSpec((tm, GQA_WIDTH), row),
                  pl.BlockSpec((1, GQA_WIDTH), const),
                  pl.BlockSpec((d, d), const, pipeline_mode=pl.Buffered(1)),
                  pl.BlockSpec((tm, d), row),
                  _mod_spec(ms, 2, tpm),
                  pl.BlockSpec((1, d), const),
                  _mod_spec(ms, 3, tpm), _mod_spec(ms, 4, tpm)],
        out_specs=[pl.BlockSpec((tm, d), row), pl.BlockSpec((tm, d), row)],
        out_shape=[jax.ShapeDtypeStruct((m, d), F32), jax.ShapeDtypeStruct((m, d), BF16)],
        compiler_params=_cparams("parallel"),
        name="out_proj",
    )(o_da, o_pool, o_gqa, gqa_out_norm.reshape(1, GQA_WIDTH), w_out_bf, x, g1,
      norm_ffn.reshape(1, d), sh2, sc2)


def _ffn_kernel(*refs, tm, tiles_per_seq, seq_starts, final):
    (h_ref, hp_ref, hn_ref, wg_ref, wv_ref, cwg_ref, cwv_ref, cbg_ref, cbv_ref, wd_ref,
     x_ref, g2_ref) = refs[:12]
    pos = 12
    if final:
        fn_ref = refs[pos]
        pos += 1
    o_ref, lhs_ref, uga_ref, ugb_ref, uva_ref, uvb_ref = refs[pos:]
    i = pl.program_id(0)
    j = pl.program_id(1)

    @pl.when(j == 0)
    def _():
        first = (i % tiles_per_seq) == 0
        last = (i % tiles_per_seq) == tiles_per_seq - 1
        lhs_ref[0:HALO, :] = jnp.where(first, jnp.zeros_like(hp_ref[...]), hp_ref[...])
        lhs_ref[HALO:HALO + tm, :] = h_ref[...]
        lhs_ref[HALO + tm:, :] = jnp.where(last, jnp.zeros_like(hn_ref[...]), hn_ref[...])
        o_ref[...] = jnp.zeros_like(o_ref)

    lhs = lhs_ref[...]

    def conv(u_ref, cw, cb):
        def taps(start, n, fix=None):
            prev, nxt = u_ref[pl.ds(HALO + start - 1, n), :], u_ref[pl.ds(HALO + start + 1, n), :]
            if fix is not None:
                row = lax.broadcasted_iota(jnp.int32, prev.shape, 0) + start
                prev = jnp.where(row == fix, 0.0, prev)
                nxt = jnp.where(row == fix - 1, 0.0, nxt)
            return prev * cw[0:1] + u_ref[pl.ds(HALO + start, n), :] * cw[1:2] + nxt * cw[2:3] + cb

        c = taps(0, tm)
        for b in seq_starts:
            c = jnp.concatenate([c[:b - 8], taps(b - 8, 16, fix=b), c[b + 8:]], axis=0)
        return c

    tf = wg_ref.shape[1]
    hf = tf // 2
    uga_ref[...] = _dot(lhs, wg_ref[:, :hf])
    ugb_ref[...] = _dot(lhs, wg_ref[:, hf:])
    uva_ref[...] = _dot(lhs, wv_ref[:, :hf])
    uvb_ref[...] = _dot(lhs, wv_ref[:, hf:])
    gate_a = _silu(conv(uga_ref, cwg_ref[:, :hf], cbg_ref[:, :hf]))
    gate_b = _silu(conv(ugb_ref, cwg_ref[:, hf:], cbg_ref[:, hf:]))
    act_a = (gate_a * conv(uva_ref, cwv_ref[:, :hf], cbv_ref[:, :hf])).astype(BF16)
    act_b = (gate_b * conv(uvb_ref, cwv_ref[:, hf:], cbv_ref[:, hf:])).astype(BF16)
    o_ref[...] += _dot(act_a, wd_ref[:hf, :]) + _dot(act_b, wd_ref[hf:, :])

    @pl.when(j == pl.num_programs(1) - 1)
    def _():
        out = x_ref[...] + g2_ref[0] * o_ref[...]
        if final:
            out = _rms(out, fn_ref[...])
        o_ref[...] = out


def _ffn_call(h2, w_up_bf, conv_w, conv_b, w_down_bf, x, ms, final_norm, tm, seq, rows_per_mod):
    m, d = x.shape
    tf = FFN_TF
    nj = D_FF // tf
    assert seq % tm == 0 or tm % seq == 0
    tps = max(seq // tm, 1)
    seq_starts = tuple(range(seq, tm, seq))
    tpm = rows_per_mod // tm
    hb = tm // HALO
    nhb = m // HALO
    final = final_norm is not None
    row = lambda i, j: (i, 0)
    gate_col = lambda i, j: (0, j)
    val_col = lambda i, j: (0, j + nj)
    in_specs = [pl.BlockSpec((tm, d), row),
                pl.BlockSpec((HALO, d), lambda i, j: (jnp.maximum(i * hb - 1, 0), 0)),
                pl.BlockSpec((HALO, d), lambda i, j: (jnp.minimum((i + 1) * hb, nhb - 1), 0)),
                pl.BlockSpec((d, tf), gate_col),
                pl.BlockSpec((d, tf), val_col),
                pl.BlockSpec((CONV_W, tf), gate_col),
                pl.BlockSpec((CONV_W, tf), val_col),
                pl.BlockSpec((1, tf), gate_col),
                pl.BlockSpec((1, tf), val_col),
                pl.BlockSpec((tf, d), lambda i, j: (j, 0)),
                pl.BlockSpec((tm, d), row),
                _mod_spec(ms, 5, tpm)]
    cb = conv_b.reshape(1, 2 * D_FF)
    args = [h2, h2, h2, w_up_bf, w_up_bf, conv_w, conv_w, cb, cb, w_down_bf, x, ms[0]]
    if final:
        in_specs.append(pl.BlockSpec((1, d), lambda i, j: (0, 0)))
        args.append(final_norm.reshape(1, d))
    return pl.pallas_call(
        functools.partial(_ffn_kernel, tm=tm, tiles_per_seq=tps, seq_starts=seq_starts, final=final),
        grid=(m // tm, nj),
        in_specs=in_specs,
        out_specs=pl.BlockSpec((tm, d), row),
        out_shape=jax.ShapeDtypeStruct((m, d), F32),
        scratch_shapes=([pltpu.VMEM((tm + 2 * HALO, d), BF16)]
                        + [pltpu.VMEM((tm + 2 * HALO, tf // 2), F32)] * 4),
        compiler_params=_cparams("parallel", "arbitrary"),
        name="conv_ffn",
    )(*args)


def _rope_tables(seq, rot_dim):
    f32 = np.float32
    rows = seq // GRID_W
    row = np.repeat(np.arange(rows, dtype=f32), GRID_W)
    col = np.tile(np.arange(GRID_W, dtype=f32), rows)
    axis_dim = rot_dim // 2
    nf = axis_dim // 2
    freqs = f32(ROPE_BASE) ** (-(np.arange(nf, dtype=f32) * f32(2.0) / f32(axis_dim)))
    ar, ac = row[:, None] * freqs, col[:, None] * freqs
    cos = np.concatenate([np.cos(ar)] * 2 + [np.cos(ac)] * 2, axis=-1)
    sin = np.concatenate([np.sin(ar)] * 2 + [np.sin(ac)] * 2, axis=-1)
    first = np.tile(np.concatenate([np.ones((nf,), f32), np.zeros((nf,), f32)]), 2)
    reps = HEAD_DIM // rot_dim
    tables = (np.tile(cos, (1, reps)), np.tile(-sin * first, (1, reps)),
              np.tile(sin * (f32(1.0) - first), (1, reps)))
    return tuple(jnp.asarray(t, dtype=F32) for t in tables)


def kernel(x, c, ctx, c_ctx, w_mod, b_mod, norm_mix, norm_ffn, w_in, da_lambda_q1, da_lambda_k1,
           da_lambda_q2, da_lambda_k2, da_subln, gqa_q_norm, gqa_k_norm, pool_w, pool_scale,
           gqa_out_norm, w_out, w_up, conv_w, conv_b, w_down, final_norm):
    nb, seq, d = x.shape
    n_ctx = ctx.shape[1]
    depth = w_mod.shape[0]
    assert d == D_MODEL and seq % GRID_W == 0 and nb <= 7

    tm_lat = min(512, seq)
    tm_ctx = min(512, nb * n_ctx)
    tq_lat = min(256, seq)
    tq_ctx = min(256, n_ctx)
    tpi_lat = math.gcd(seq // tq_lat, 8)

    c8 = jnp.zeros((8, d), F32).at[:nb].set(c).at[nb].set(c_ctx)
    mods, w_in_bf = _mod_call(c8, w_mod, b_mod, [(w_in, 0)])
    mods_r = mods.reshape(depth * 8 * N_MOD, 1, d)

    rope = _rope_tables(seq, DA_HALF_DIM) + _rope_tables(seq, HEAD_DIM)

    xl = x.reshape(nb * seq, d)
    xc = ctx.reshape(nb * n_ctx, d)

    for l in range(depth):
        last = l == depth - 1
        lambda_init = 0.8 - 0.6 * math.exp(-0.3 * l)
        ms_lat = (mods_r, l, None)
        ms_ctx = (mods_r, l, nb)
        da_params = ((da_lambda_q1[l], da_lambda_k1[l], da_lambda_q2[l], da_lambda_k2[l]),
                     da_subln[l], lambda_init)

        pl_ = _inproj_call(xl, ms_lat, norm_mix[l], w_in_bf, gqa_q_norm[l], gqa_k_norm[l],
                           rope, tm_lat, seq)
        pc_ = _inproj_call(xc, ms_ctx, norm_mix[l], w_in_bf, gqa_q_norm[l], gqa_k_norm[l],
                           None, tm_ctx, nb * n_ctx)
        daq_l, dak_l, dav_l, pool_l, gq_l, gk_l, gv_l = pl_
        daq_c, dak_c, dav_c, pool_c, gq_c, gk_c, gv_c = pc_

        o_da, w_up_bf, w_down_bf = _attn_call(
            daq_l, [dak_c, dak_l], [dav_c, dav_l], [n_ctx, seq], seq, tq_lat, tpi_lat,
            DA_HEADS, 1, BF16, da_params, cast_jobs=[(w_up, l), (w_down, l)])
        o_gqa, w_out_bf, *w_in_next = _attn_call(
            gq_l, [gk_c, gk_l], [gv_c, gv_l], [n_ctx, seq], seq, tq_lat, tpi_lat,
            GQA_Q_HEADS, GQA_GROUP, F32, cast_jobs=[(w_out, l)] + ([] if last else [(w_in, l + 1)]))
        ffn_w = (w_up_bf, conv_w[l], conv_b[l], w_down_bf)
        o_pool = _pool_call(pool_l, pool_w[l], pool_scale[l], seq)
        xl, h2 = _outproj_call(o_da, o_pool, o_gqa, gqa_out_norm[l], w_out_bf, xl, ms_lat,
                               norm_ffn[l], tm_lat, seq)
        xl = _ffn_call(h2, *ffn_w, xl, ms_lat, final_norm if last else None, tm_lat, seq, seq)
        if not last:
            o_da, = _attn_call(daq_c, [dak_c], [dav_c], [n_ctx], n_ctx, tq_ctx, 1,
                               DA_HEADS, 1, BF16, da_params)
            o_gqa, = _attn_call(gq_c, [gk_c], [gv_c], [n_ctx], n_ctx, tq_ctx, 1,
                                GQA_Q_HEADS, GQA_GROUP, F32)
            o_pool = _pool_call(pool_c, pool_w[l], pool_scale[l], n_ctx)
            xc, h2 = _outproj_call(o_da, o_pool, o_gqa, gqa_out_norm[l], w_out_bf, xc, ms_ctx,
                                   norm_ffn[l], tm_ctx, nb * n_ctx)
            xc = _ffn_call(h2, *ffn_w, xc, ms_ctx, None, tm_ctx, n_ctx, nb * n_ctx)
            w_in_bf, = w_in_next

    return xl.reshape(nb, seq, d)
```

```python
import functools
import math

import jax
import jax.numpy as jnp
import numpy as np
from jax import lax
from jax.experimental import pallas as pl
from jax.experimental.pallas import tpu as pltpu

F32 = jnp.float32
BF16 = jnp.bfloat16

D_MODEL = 2048
GRID_W = 64
EPS = 1e-6
ROPE_BASE = 10000.0
N_MOD = 6
LOG2E = math.log2(math.e)

DA_HEADS = 6
DA_HALF_DIM = 64
HEAD_DIM = 128
DA_WIDTH = DA_HEADS * HEAD_DIM
POOL_WINDOWS = (2, 4, 8, 16)
POOL_WIDTH = len(POOL_WINDOWS) * HEAD_DIM
GQA_Q_HEADS = 6
GQA_KV_HEADS = 2
GQA_GROUP = GQA_Q_HEADS // GQA_KV_HEADS
GQA_WIDTH = GQA_Q_HEADS * HEAD_DIM
GQA_KV_WIDTH = GQA_KV_HEADS * HEAD_DIM
IN_COLS = 3 * DA_WIDTH + POOL_WIDTH + GQA_WIDTH + 2 * GQA_KV_WIDTH
D_FF = 5632
CONV_W = 3

OFF_DAQ = 0
OFF_DAK = OFF_DAQ + DA_WIDTH
OFF_DAV = OFF_DAK + DA_WIDTH
OFF_POOL = OFF_DAV + DA_WIDTH
OFF_GQ = OFF_POOL + POOL_WIDTH
OFF_GK = OFF_GQ + GQA_WIDTH
OFF_GV = OFF_GK + GQA_KV_WIDTH

V7X_VMEM_LIMIT_BYTES = 56 * 1024 * 1024
HALO = 16
FFN_TF = 512
PROJ_SUB = 256
CAST_STEPS = 16
POOL_PAD = 16


def _cparams(*sem):
    return pltpu.CompilerParams(dimension_semantics=sem, vmem_limit_bytes=V7X_VMEM_LIMIT_BYTES)


def _dot(a, b):
    return jnp.dot(a, b, preferred_element_type=F32)


def _dot_nt(a, b):
    return lax.dot_general(a, b, (((1,), (1,)), ((), ())), preferred_element_type=F32)


def _rms(x, gain):
    return x * lax.rsqrt(jnp.mean(x * x, axis=-1, keepdims=True) + EPS) * gain


def _modulate(x, gain, shift, scale):
    return _rms(x, gain) * (1.0 + scale) + shift


def _silu(x):
    return x / (1.0 + jnp.exp2(x * -LOG2E))


def _cast_specs(jobs, step_of, grid_steps):
    in_specs, out_specs, out_shapes, args = [], [], [], []
    n = min(CAST_STEPS, 1 << (grid_steps.bit_length() - 1))
    for w, layer in jobs:
        _, r, c = w.shape
        rb = r // n
        assert rb * n == r and rb % HALO == 0
        blk = lambda *g: jnp.minimum(step_of(*g), n - 1)
        in_specs.append(pl.BlockSpec((None, rb, c), lambda *g, layer=layer, blk=blk: (layer, blk(*g), 0)))
        out_specs.append(pl.BlockSpec((rb, c), lambda *g, blk=blk: (blk(*g), 0)))
        out_shapes.append(jax.ShapeDtypeStruct((r, c), BF16))
        args.append(w)
    return in_specs, out_specs, out_shapes, args


def _run_casts(in_refs, out_refs):
    for src, dst in zip(in_refs, out_refs):
        dst[...] = src[...].astype(BF16)


def _mod_kernel(*refs, n_cast):
    c_ref, w_ref, b_ref = refs[:3]
    cast_in = refs[3:3 + n_cast]
    o_ref = refs[3 + n_cast]
    cast_out = refs[4 + n_cast:]
    s = _silu(c_ref[...]).astype(BF16)
    o_ref[0] = _dot(s, w_ref[0].astype(BF16)) + b_ref[0]
    _run_casts(cast_in, cast_out)


def _mod_call(c8, w_mod, b_mod, cast_jobs, tn=1024):
    depth, d, n = w_mod.shape
    nj = n // tn
    c_in, c_out, c_shapes, c_args = _cast_specs(cast_jobs, lambda l, j: l * nj + j, depth * nj)
    return pl.pallas_call(
        functools.partial(_mod_kernel, n_cast=len(cast_jobs)),
        grid=(depth, nj),
        in_specs=[pl.BlockSpec((8, d), lambda l, j: (0, 0)),
                  pl.BlockSpec((1, d, tn), lambda l, j: (l, 0, j)),
                  pl.BlockSpec((1, 1, tn), lambda l, j: (l, 0, j))] + c_in,
        out_specs=[pl.BlockSpec((1, 8, tn), lambda l, j: (l, 0, j))] + c_out,
        out_shape=[jax.ShapeDtypeStruct((depth, 8, n), F32)] + c_shapes,
        compiler_params=_cparams("arbitrary", "arbitrary"),
        name="adaln_mod",
    )(c8, w_mod, b_mod.reshape(depth, 1, n), *c_args)


def _rope(x, c, s1, s2, shift):
    return x * c + pltpu.roll(x, HEAD_DIM - shift, 1) * s1 + pltpu.roll(x, shift, 1) * s2


def _inproj_kernel(*refs, use_rope, sub):
    x_ref, sh_ref, sc_ref, nrm_ref, w_ref, qn_ref, kn_ref = refs[:7]
    pos = 7
    rope_refs = ()
    if use_rope:
        rope_refs = refs[pos:pos + 6]
        pos += 6
    daq_ref, dak_ref, dav_ref, pool_ref, gq_ref, gk_ref, gv_ref = refs[pos:]

    def head_cols(i):
        return slice(i * HEAD_DIM, (i + 1) * HEAD_DIM)

    for r in range(x_ref.shape[0] // sub):
        rows = slice(r * sub, (r + 1) * sub)
        h = _modulate(x_ref[rows, :], nrm_ref[...], sh_ref[0], sc_ref[0]).astype(BF16)
        dac, das1, das2, gc, gs1, gs2 = (t[rows, :] for t in rope_refs) if use_rope else (None,) * 6

        def proj(off, width):
            return _dot(h, w_ref[:, off:off + width])

        a = proj(OFF_DAQ, DA_WIDTH)
        for i in range(DA_HEADS):
            t = a[:, head_cols(i)]
            if use_rope:
                t = _rope(t, dac, das1, das2, DA_HALF_DIM // 4)
            daq_ref[rows, head_cols(i)] = (t * (DA_HALF_DIM ** -0.5 * LOG2E)).astype(BF16)
        a = proj(OFF_DAK, DA_WIDTH)
        for i in range(DA_HEADS):
            t = a[:, head_cols(i)]
            if use_rope:
                t = _rope(t, dac, das1, das2, DA_HALF_DIM // 4)
            dak_ref[rows, head_cols(i)] = t.astype(BF16)
        dav_ref[rows, :] = proj(OFF_DAV, DA_WIDTH).astype(BF16)
        pool_ref[rows, :] = proj(OFF_POOL, POOL_WIDTH)
        a = proj(OFF_GQ, GQA_WIDTH)
        for i in range(GQA_Q_HEADS):
            t = _rms(a[:, head_cols(i)], qn_ref[...])
            if use_rope:
                t = _rope(t, gc, gs1, gs2, HEAD_DIM // 4)
            gq_ref[rows, head_cols(i)] = (t * (HEAD_DIM ** -0.5 * LOG2E)).astype(BF16)
        a = proj(OFF_GK, GQA_KV_WIDTH)
        for i in range(GQA_KV_HEADS):
            t = _rms(a[:, head_cols(i)], kn_ref[...])
            if use_rope:
                t = _rope(t, gc, gs1, gs2, HEAD_DIM // 4)
            gk_ref[rows, head_cols(i)] = t.astype(BF16)
        gv_ref[rows, :] = proj(OFF_GV, GQA_KV_WIDTH).astype(BF16)


def _mod_spec(ms, chunk, tpm):
    _, layer, fixed_row = ms

    def index(i, *_):
        r = i // tpm if fixed_row is None else fixed_row
        return ((layer * 8 + r) * N_MOD + chunk, 0, 0)

    return pl.BlockSpec((1, 1, D_MODEL), index)


def _inproj_call(x, ms, norm_gain, w_in_bf, q_norm, k_norm, rope, tm, rows_per_mod):
    m, d = x.shape
    tpm = rows_per_mod // tm
    row = lambda i: (i, 0)
    const = lambda i: (0, 0)
    shift, scale = ms[0], ms[0]
    in_specs = [pl.BlockSpec((tm, d), row),
                _mod_spec(ms, 0, tpm),
                _mod_spec(ms, 1, tpm),
                pl.BlockSpec((1, d), const),
                pl.BlockSpec((d, IN_COLS), const, pipeline_mode=pl.Buffered(1)),
                pl.BlockSpec((1, HEAD_DIM), const),
                pl.BlockSpec((1, HEAD_DIM), const)]
    args = [x, shift, scale, norm_gain.reshape(1, d), w_in_bf,
            q_norm.reshape(1, HEAD_DIM), k_norm.reshape(1, HEAD_DIM)]
    if rope is not None:
        tps = rope[0].shape[0] // tm
        in_specs += [pl.BlockSpec((tm, HEAD_DIM), lambda i: (i % tps, 0))] * 6
        args += list(rope)
    widths = (DA_WIDTH, DA_WIDTH, DA_WIDTH, POOL_WIDTH, GQA_WIDTH, GQA_KV_WIDTH, GQA_KV_WIDTH)
    dtypes = (BF16, BF16, BF16, F32, BF16, BF16, BF16)
    return pl.pallas_call(
        functools.partial(_inproj_kernel, use_rope=rope is not None, sub=min(tm, PROJ_SUB)),
        grid=(m // tm,),
        in_specs=in_specs,
        out_specs=[pl.BlockSpec((tm, w), row) for w in widths],
        out_shape=[jax.ShapeDtypeStruct((m, w), dt) for w, dt in zip(widths, dtypes)],
        compiler_params=_cparams("parallel"),
        name="in_proj",
    )(*args)


def _attn_kernel(*refs, nseg, tq, tiles_per_iter, lambda_init, n_cast):
    da = lambda_init is not None
    q_ref = refs[0]
    k_refs = refs[1:1 + nseg]
    v_refs = refs[1 + nseg:1 + 2 * nseg]
    pos = 1 + 2 * nseg
    if da:
        lq1, lk1, lq2, lk2, subln_ref = refs[pos:pos + 5]
        pos += 5
    cast_in = refs[pos:pos + n_cast]
    o_ref = refs[pos + n_cast]
    cast_out = refs[pos + n_cast + 1:pos + 2 * n_cast + 1]
    kcat, vcat = refs[pos + 2 * n_cast + 1:]
    _run_casts(cast_in, cast_out)

    n_kv = kcat.shape[0]
    n_q = q_ref.shape[1] // HEAD_DIM
    for kv in range(n_kv):
        cols = slice(kv * HEAD_DIM, (kv + 1) * HEAD_DIM)
        off = 0
        for kr, vr in zip(k_refs, v_refs):
            n = kr.shape[0]
            kcat[kv, off:off + n, :] = kr[:, cols]
            vcat[kv, off:off + n, :HEAD_DIM] = vr[:, cols]
            off += n
        vcat[kv, :, HEAD_DIM:] = jnp.ones((off, HEAD_DIM), BF16)

    def attend(q, kv):
        s = _dot_nt(q, kcat[kv])
        e = jnp.exp2(s - jnp.max(s, axis=-1, keepdims=True)).astype(BF16)
        o = _dot(e, vcat[kv])
        return o[:, :HEAD_DIM] / o[:, HEAD_DIM:]

    if da:
        lam = (jnp.exp(jnp.sum(lq1[...] * lk1[...], axis=-1, keepdims=True))
               - jnp.exp(jnp.sum(lq2[...] * lk2[...], axis=-1, keepdims=True)) + lambda_init)
        subln = subln_ref[...]

    def tile(t):
        rows = pl.ds(pl.multiple_of(t * tq, tq), tq)
        for g in range(n_q):
            cols = slice(g * HEAD_DIM, (g + 1) * HEAD_DIM)
            kv = g * n_kv // n_q
            q = q_ref[rows, cols]
            if da:
                lane = lax.broadcasted_iota(jnp.int32, q.shape, 1)
                zero = jnp.zeros_like(q)
                a1 = attend(jnp.where(lane < DA_HALF_DIM, q, zero), kv)
                a2 = attend(jnp.where(lane >= DA_HALF_DIM, q, zero), kv)
                o = _rms(a1 - lam * a2, subln) * (1.0 - lambda_init)
            else:
                o = attend(q, kv)
            o_ref[rows, cols] = o.astype(o_ref.dtype)

    n_iter = q_ref.shape[0] // (tq * tiles_per_iter)

    def body(it, carry):
        for u in range(tiles_per_iter):
            tile(it * tiles_per_iter + u)
        return carry

    lax.fori_loop(0, n_iter, body, 0)


def _attn_call(q, k_segs, v_segs, seg_lens, sq, tq, tiles_per_iter, n_heads, q_per_kv, kv_per_step,
               out_dtype, da_params=None, cast_jobs=()):
    mq = q.shape[0]
    nb = mq // sq
    nseg = len(k_segs)
    sk = sum(seg_lens)
    n_steps = n_heads // (q_per_kv * kv_per_step)
    c_in, c_out, c_shapes, c_args = _cast_specs(cast_jobs, lambda b, h: b * n_steps + h, nb * n_steps)
    qspec = pl.BlockSpec((sq, kv_per_step * q_per_kv * HEAD_DIM), lambda b, h: (b, h))
    kv_specs = [pl.BlockSpec((n, kv_per_step * HEAD_DIM), lambda b, h: (b, h)) for n in seg_lens]
    in_specs = [qspec] + kv_specs + kv_specs
    args = [q, *k_segs, *v_segs]
    lambda_init = None
    if da_params is not None:
        lam_vecs, subln, lambda_init = da_params
        vec = lambda n: pl.BlockSpec((1, n), lambda b, h: (0, 0))
        in_specs += [vec(DA_HALF_DIM)] * 4 + [vec(HEAD_DIM)]
        args += [v.reshape(1, DA_HALF_DIM) for v in lam_vecs] + [subln.reshape(1, HEAD_DIM)]
    return pl.pallas_call(
        functools.partial(_attn_kernel, nseg=nseg, tq=tq, tiles_per_iter=tiles_per_iter,
                          lambda_init=lambda_init, n_cast=len(cast_jobs)),
        grid=(nb, n_steps),
        in_specs=in_specs + c_in,
        out_specs=[qspec] + c_out,
        out_shape=[jax.ShapeDtypeStruct((mq, n_heads * HEAD_DIM), out_dtype)] + c_shapes,
        scratch_shapes=[pltpu.VMEM((kv_per_step, sk, HEAD_DIM), BF16),
                        pltpu.VMEM((kv_per_step, sk, 2 * HEAD_DIM), BF16)],
        compiler_params=_cparams("arbitrary", "arbitrary"),
        name="da_attn" if da_params is not None else "gqa_attn",
    )(*args, *c_args)


def _pool_kernel(u_ref, w_ref, ps_ref, o_ref):
    t = u_ref.shape[0]
    row = lax.broadcasted_iota(jnp.int32, (t, HEAD_DIM), 0)
    for g, w in enumerate(POOL_WINDOWS):
        sl = slice(g * HEAD_DIM, (g + 1) * HEAD_DIM)
        u = u_ref[:, sl]
        zeros = jnp.zeros((POOL_PAD, HEAD_DIM), F32)
        ue = jnp.concatenate([zeros, u, zeros], axis=0)
        n = t + 2 * POOL_PAD
        acc = ue + pltpu.roll(ue, 1, 0)
        m = 2
        while m < w:
            acc = pltpu.roll(acc, m // 2, 0) + pltpu.roll(acc, n - m // 2, 0)
            m *= 2
        cnt = jnp.minimum(row - w // 2 + w, t) - jnp.maximum(row - w // 2, 0)
        pooled = acc[POOL_PAD:POOL_PAD + t] / cnt.astype(F32) - u
        mixed = _dot(pooled.astype(BF16), w_ref[g].astype(BF16))
        o_ref[:, sl] = (mixed * ps_ref[:, sl]).astype(BF16)


def _pool_call(u, w_pool, pool_scale, seq):
    m = u.shape[0]
    return pl.pallas_call(
        _pool_kernel,
        grid=(m // seq,),
        in_specs=[pl.BlockSpec((seq, POOL_WIDTH), lambda b: (b, 0)),
                  pl.BlockSpec(w_pool.shape, lambda b: (0, 0, 0)),
                  pl.BlockSpec((1, POOL_WIDTH), lambda b: (0, 0))],
        out_specs=pl.BlockSpec((seq, POOL_WIDTH), lambda b: (b, 0)),
        out_shape=jax.ShapeDtypeStruct((m, POOL_WIDTH), BF16),
        compiler_params=_cparams("parallel"),
        name="pool_mixer",
    )(u, w_pool, pool_scale.reshape(1, POOL_WIDTH))


def _outproj_kernel(oda_ref, opool_ref, ogqa_ref, gn_ref, w_ref, x_ref, g1_ref,
                    nrm2_ref, sh2_ref, sc2_ref, xo_ref, h2_ref, *, sub):
    for r in range(x_ref.shape[0] // sub):
        rows = slice(r * sub, (r + 1) * sub)
        ogn = _rms(ogqa_ref[rows, :], gn_ref[...]).astype(BF16)
        cat = jnp.concatenate([oda_ref[rows, :], opool_ref[rows, :], ogn], axis=1)
        xn = x_ref[rows, :] + g1_ref[0] * _dot(cat, w_ref[...])
        xo_ref[rows, :] = xn
        h2_ref[rows, :] = _modulate(xn, nrm2_ref[...], sh2_ref[0], sc2_ref[0]).astype(BF16)


def _outproj_call(o_da, o_pool, o_gqa, gqa_out_norm, w_out_bf, x, ms, norm_ffn, tm, rows_per_mod):
    m, d = x.shape
    tpm = rows_per_mod // tm
    row = lambda i: (i, 0)
    const = lambda i: (0, 0)
    g1 = sh2 = sc2 = ms[0]
    return pl.pallas_call(
        functools.partial(_outproj_kernel, sub=min(tm, PROJ_SUB)),
        grid=(m // tm,),
        in_specs=[pl.BlockSpec((tm, DA_WIDTH), row),
                  pl.BlockSpec((tm, POOL_WIDTH), row),
                  pl.BlockSpec((tm, GQA_WIDTH), row),
                  pl.BlockSpec((1, GQA_WIDTH), const),
                  pl.BlockSpec((d, d), const, pipeline_mode=pl.Buffered(1)),
                  pl.BlockSpec((tm, d), row),
                  _mod_spec(ms, 2, tpm),
                  pl.BlockSpec((1, d), const),
                  _mod_spec(ms, 3, tpm), _mod_spec(ms, 4, tpm)],
        out_specs=[pl.BlockSpec((tm, d), row), pl.BlockSpec((tm, d), row)],
        out_shape=[jax.ShapeDtypeStruct((m, d), F32), jax.ShapeDtypeStruct((m, d), BF16)],
        compiler_params=_cparams("parallel"),
        name="out_proj",
    )(o_da, o_pool, o_gqa, gqa_out_norm.reshape(1, GQA_WIDTH), w_out_bf, x, g1,
      norm_ffn.reshape(1, d), sh2, sc2)


def _ffn_kernel(*refs, tm, tiles_per_seq, seq_starts, final):
    (h_ref, hp_ref, hn_ref, wg_ref, wv_ref, cwg_ref, cwv_ref, cbg_ref, cbv_ref, wd_ref,
     x_ref, g2_ref) = refs[:12]
    pos = 12
    if final:
        fn_ref = refs[pos]
        pos += 1
    o_ref, lhs_ref, uga_ref, ugb_ref, uva_ref, uvb_ref = refs[pos:]
    i = pl.program_id(0)
    j = pl.program_id(1)

    @pl.when(j == 0)
    def _():
        first = (i % tiles_per_seq) == 0
        last = (i % tiles_per_seq) == tiles_per_seq - 1
        lhs_ref[0:HALO, :] = jnp.where(first, jnp.zeros_like(hp_ref[...]), hp_ref[...])
        lhs_ref[HALO:HALO + tm, :] = h_ref[...]
        lhs_ref[HALO + tm:, :] = jnp.where(last, jnp.zeros_like(hn_ref[...]), hn_ref[...])
        o_ref[...] = jnp.zeros_like(o_ref)

    lhs = lhs_ref[...]

    def conv(u_ref, cw, cb):
        def taps(start, n, fix=None):
            prev, nxt = u_ref[pl.ds(HALO + start - 1, n), :], u_ref[pl.ds(HALO + start + 1, n), :]
            if fix is not None:
                row = lax.broadcasted_iota(jnp.int32, prev.shape, 0) + start
                prev = jnp.where(row == fix, 0.0, prev)
                nxt = jnp.where(row == fix - 1, 0.0, nxt)
            return prev * cw[0:1] + u_ref[pl.ds(HALO + start, n), :] * cw[1:2] + nxt * cw[2:3] + cb

        c = taps(0, tm)
        for b in seq_starts:
            c = jnp.concatenate([c[:b - 8], taps(b - 8, 16, fix=b), c[b + 8:]], axis=0)
        return c

    tf = wg_ref.shape[1]
    hf = tf // 2
    uga_ref[...] = _dot(lhs, wg_ref[:, :hf])
    ugb_ref[...] = _dot(lhs, wg_ref[:, hf:])
    uva_ref[...] = _dot(lhs, wv_ref[:, :hf])
    uvb_ref[...] = _dot(lhs, wv_ref[:, hf:])
    gate_a = _silu(conv(uga_ref, cwg_ref[:, :hf], cbg_ref[:, :hf]))
    gate_b = _silu(conv(ugb_ref, cwg_ref[:, hf:], cbg_ref[:, hf:]))
    act_a = (gate_a * conv(uva_ref, cwv_ref[:, :hf], cbv_ref[:, :hf])).astype(BF16)
    act_b = (gate_b * conv(uvb_ref, cwv_ref[:, hf:], cbv_ref[:, hf:])).astype(BF16)
    o_ref[...] += _dot(act_a, wd_ref[:hf, :]) + _dot(act_b, wd_ref[hf:, :])

    @pl.when(j == pl.num_programs(1) - 1)
    def _():
        out = x_ref[...] + g2_ref[0] * o_ref[...]
        if final:
            out = _rms(out, fn_ref[...])
        o_ref[...] = out


def _ffn_call(h2, w_up_bf, conv_w, conv_b, w_down_bf, x, ms, final_norm, tm, seq, rows_per_mod):
    m, d = x.shape
    tf = FFN_TF
    nj = D_FF // tf
    assert seq % tm == 0 or tm % seq == 0
    tps = max(seq // tm, 1)
    seq_starts = tuple(range(seq, tm, seq))
    tpm = rows_per_mod // tm
    hb = tm // HALO
    nhb = m // HALO
    final = final_norm is not None
    row = lambda i, j: (i, 0)
    gate_col = lambda i, j: (0, j)
    val_col = lambda i, j: (0, j + nj)
    in_specs = [pl.BlockSpec((tm, d), row),
                pl.BlockSpec((HALO, d), lambda i, j: (jnp.maximum(i * hb - 1, 0), 0)),
                pl.BlockSpec((HALO, d), lambda i, j: (jnp.minimum((i + 1) * hb, nhb - 1), 0)),
                pl.BlockSpec((d, tf), gate_col),
                pl.BlockSpec((d, tf), val_col),
                pl.BlockSpec((CONV_W, tf), gate_col),
                pl.BlockSpec((CONV_W, tf), val_col),
                pl.BlockSpec((1, tf), gate_col),
                pl.BlockSpec((1, tf), val_col),
                pl.BlockSpec((tf, d), lambda i, j: (j, 0)),
                pl.BlockSpec((tm, d), row),
                _mod_spec(ms, 5, tpm)]
    cb = conv_b.reshape(1, 2 * D_FF)
    args = [h2, h2, h2, w_up_bf, w_up_bf, conv_w, conv_w, cb, cb, w_down_bf, x, ms[0]]
    if final:
        in_specs.append(pl.BlockSpec((1, d), lambda i, j: (0, 0)))
        args.append(final_norm.reshape(1, d))
    return pl.pallas_call(
        functools.partial(_ffn_kernel, tm=tm, tiles_per_seq=tps, seq_starts=seq_starts, final=final),
        grid=(m // tm, nj),
        in_specs=in_specs,
        out_specs=pl.BlockSpec((tm, d), row),
        out_shape=jax.ShapeDtypeStruct((m, d), F32),
        scratch_shapes=([pltpu.VMEM((tm + 2 * HALO, d), BF16)]
                        + [pltpu.VMEM((tm + 2 * HALO, tf // 2), F32)] * 4),
        compiler_params=_cparams("parallel", "arbitrary"),
        name="conv_ffn",
    )(*args)


def _rope_tables(seq, rot_dim):
    f32 = np.float32
    rows = seq // GRID_W
    row = np.repeat(np.arange(rows, dtype=f32), GRID_W)
    col = np.tile(np.arange(GRID_W, dtype=f32), rows)
    axis_dim = rot_dim // 2
    nf = axis_dim // 2
    freqs = f32(ROPE_BASE) ** (-(np.arange(nf, dtype=f32) * f32(2.0) / f32(axis_dim)))
    ar, ac = row[:, None] * freqs, col[:, None] * freqs
    cos = np.concatenate([np.cos(ar)] * 2 + [np.cos(ac)] * 2, axis=-1)
    sin = np.concatenate([np.sin(ar)] * 2 + [np.sin(ac)] * 2, axis=-1)
    first = np.tile(np.concatenate([np.ones((nf,), f32), np.zeros((nf,), f32)]), 2)
    reps = HEAD_DIM // rot_dim
    tables = (np.tile(cos, (1, reps)), np.tile(-sin * first, (1, reps)),
              np.tile(sin * (f32(1.0) - first), (1, reps)))
    return tuple(jnp.asarray(t, dtype=F32) for t in tables)


def kernel(x, c, ctx, c_ctx, w_mod, b_mod, norm_mix, norm_ffn, w_in, da_lambda_q1, da_lambda_k1,
           da_lambda_q2, da_lambda_k2, da_subln, gqa_q_norm, gqa_k_norm, pool_w, pool_scale,
           gqa_out_norm, w_out, w_up, conv_w, conv_b, w_down, final_norm):
    nb, seq, d = x.shape
    n_ctx = ctx.shape[1]
    depth = w_mod.shape[0]
    assert d == D_MODEL and seq % GRID_W == 0 and nb <= 7

    tm_lat = min(512, seq)
    tm_ctx = min(512, nb * n_ctx)
    tq_lat = min(128, seq)
    tq_ctx = min(256, n_ctx)
    tpi_lat = math.gcd(seq // tq_lat, 16)

    c8 = jnp.zeros((8, d), F32).at[:nb].set(c).at[nb].set(c_ctx)
    mods, w_in_bf = _mod_call(c8, w_mod, b_mod, [(w_in, 0)])
    mods_r = mods.reshape(depth * 8 * N_MOD, 1, d)

    rope = _rope_tables(seq, DA_HALF_DIM) + _rope_tables(seq, HEAD_DIM)

    xl = x.reshape(nb * seq, d)
    xc = ctx.reshape(nb * n_ctx, d)

    for l in range(depth):
        last = l == depth - 1
        lambda_init = 0.8 - 0.6 * math.exp(-0.3 * l)
        ms_lat = (mods_r, l, None)
        ms_ctx = (mods_r, l, nb)
        da_params = ((da_lambda_q1[l], da_lambda_k1[l], da_lambda_q2[l], da_lambda_k2[l]),
                     da_subln[l], lambda_init)

        pl_ = _inproj_call(xl, ms_lat, norm_mix[l], w_in_bf, gqa_q_norm[l], gqa_k_norm[l],
                           rope, tm_lat, seq)
        pc_ = _inproj_call(xc, ms_ctx, norm_mix[l], w_in_bf, gqa_q_norm[l], gqa_k_norm[l],
                           None, tm_ctx, nb * n_ctx)
        daq_l, dak_l, dav_l, pool_l, gq_l, gk_l, gv_l = pl_
        daq_c, dak_c, dav_c, pool_c, gq_c, gk_c, gv_c = pc_

        o_da, w_up_bf, w_down_bf = _attn_call(
            daq_l, [dak_c, dak_l], [dav_c, dav_l], [n_ctx, seq], seq, tq_lat, tpi_lat,
            DA_HEADS, 1, 1, BF16, da_params, cast_jobs=[(w_up, l), (w_down, l)])
        o_gqa, w_out_bf, *w_in_next = _attn_call(
            gq_l, [gk_c, gk_l], [gv_c, gv_l], [n_ctx, seq], seq, tq_lat, tpi_lat,
            GQA_Q_HEADS, GQA_GROUP, 1, F32, cast_jobs=[(w_out, l)] + ([] if last else [(w_in, l + 1)]))
        ffn_w = (w_up_bf, conv_w[l], conv_b[l], w_down_bf)
        o_pool = _pool_call(pool_l, pool_w[l], pool_scale[l], seq)
        xl, h2 = _outproj_call(o_da, o_pool, o_gqa, gqa_out_norm[l], w_out_bf, xl, ms_lat,
                               norm_ffn[l], tm_lat, seq)
        xl = _ffn_call(h2, *ffn_w, xl, ms_lat, final_norm if last else None, tm_lat, seq, seq)
        if not last:
            o_da, = _attn_call(daq_c, [dak_c], [dav_c], [n_ctx], n_ctx, tq_ctx, 1,
                               DA_HEADS, 1, DA_HEADS, BF16, da_params)
            o_gqa, = _attn_call(gq_c, [gk_c], [gv_c], [n_ctx], n_ctx, tq_ctx, 1,
                                GQA_Q_HEADS, GQA_GROUP, GQA_KV_HEADS, F32)
            o_pool = _pool_call(pool_c, pool_w[l], pool_scale[l], n_ctx)
            xc, h2 = _outproj_call(o_da, o_pool, o_gqa, gqa_out_norm[l], w_out_bf, xc, ms_ctx,
                                   norm_ffn[l], tm_ctx, nb * n_ctx)
            xc = _ffn_call(h2, *ffn_w, xc, ms_ctx, None, tm_ctx, n_ctx, nb * n_ctx)
            w_in_bf, = w_in_next

    return xl.reshape(nb, seq, d)
```

```python
import functools
import math

import jax
import jax.numpy as jnp
import numpy as np
from jax import lax
from jax.experimental import pallas as pl
from jax.experimental.pallas import tpu as pltpu

F32 = jnp.float32
BF16 = jnp.bfloat16

D_MODEL = 2048
GRID_W = 64
EPS = 1e-6
ROPE_BASE = 10000.0
N_MOD = 6
LOG2E = math.log2(math.e)

DA_HEADS = 6
DA_HALF_DIM = 64
HEAD_DIM = 128
DA_WIDTH = DA_HEADS * HEAD_DIM
POOL_WINDOWS = (2, 4, 8, 16)
POOL_WIDTH = len(POOL_WINDOWS) * HEAD_DIM
GQA_Q_HEADS = 6
GQA_KV_HEADS = 2
GQA_GROUP = GQA_Q_HEADS // GQA_KV_HEADS
GQA_WIDTH = GQA_Q_HEADS * HEAD_DIM
GQA_KV_WIDTH = GQA_KV_HEADS * HEAD_DIM
IN_COLS = 3 * DA_WIDTH + POOL_WIDTH + GQA_WIDTH + 2 * GQA_KV_WIDTH
D_FF = 5632
CONV_W = 3

OFF_DAQ = 0
OFF_DAK = OFF_DAQ + DA_WIDTH
OFF_DAV = OFF_DAK + DA_WIDTH
OFF_POOL = OFF_DAV + DA_WIDTH
OFF_GQ = OFF_POOL + POOL_WIDTH
OFF_GK = OFF_GQ + GQA_WIDTH
OFF_GV = OFF_GK + GQA_KV_WIDTH

V7X_VMEM_LIMIT_BYTES = 56 * 1024 * 1024
HALO = 16
FFN_TF = 512
PROJ_SUB = 256
CAST_STEPS = 16
POOL_PAD = 16


def _cparams(*sem):
    return pltpu.CompilerParams(dimension_semantics=sem, vmem_limit_bytes=V7X_VMEM_LIMIT_BYTES)


def _dot(a, b):
    return jnp.dot(a, b, preferred_element_type=F32)


def _dot_nt(a, b):
    return lax.dot_general(a, b, (((1,), (1,)), ((), ())), preferred_element_type=F32)


def _rms(x, gain):
    return x * lax.rsqrt(jnp.mean(x * x, axis=-1, keepdims=True) + EPS) * gain


def _modulate(x, gain, shift, scale):
    return _rms(x, gain) * (1.0 + scale) + shift


def _silu(x):
    return x / (1.0 + jnp.exp2(x * -LOG2E))


def _cast_specs(jobs, step_of, grid_steps):
    in_specs, out_specs, out_shapes, args = [], [], [], []
    n = min(CAST_STEPS, 1 << (grid_steps.bit_length() - 1))
    for w, layer in jobs:
        _, r, c = w.shape
        rb = r // n
        assert rb * n == r and rb % HALO == 0
        blk = lambda *g: jnp.minimum(step_of(*g), n - 1)
        in_specs.append(pl.BlockSpec((None, rb, c), lambda *g, layer=layer, blk=blk: (layer, blk(*g), 0)))
        out_specs.append(pl.BlockSpec((rb, c), lambda *g, blk=blk: (blk(*g), 0)))
        out_shapes.append(jax.ShapeDtypeStruct((r, c), BF16))
        args.append(w)
    return in_specs, out_specs, out_shapes, args


def _run_casts(in_refs, out_refs):
    for src, dst in zip(in_refs, out_refs):
        dst[...] = src[...].astype(BF16)


def _mod_kernel(*refs, n_cast):
    c_ref, w_ref, b_ref = refs[:3]
    cast_in = refs[3:3 + n_cast]
    o_ref = refs[3 + n_cast]
    cast_out = refs[4 + n_cast:]
    s = _silu(c_ref[...]).astype(BF16)
    o_ref[0] = _dot(s, w_ref[0].astype(BF16)) + b_ref[0]
    _run_casts(cast_in, cast_out)


def _mod_call(c8, w_mod, b_mod, cast_jobs, tn=1024):
    depth, d, n = w_mod.shape
    nj = n // tn
    c_in, c_out, c_shapes, c_args = _cast_specs(cast_jobs, lambda l, j: l * nj + j, depth * nj)
    return pl.pallas_call(
        functools.partial(_mod_kernel, n_cast=len(cast_jobs)),
        grid=(depth, nj),
        in_specs=[pl.BlockSpec((8, d), lambda l, j: (0, 0)),
                  pl.BlockSpec((1, d, tn), lambda l, j: (l, 0, j)),
                  pl.BlockSpec((1, 1, tn), lambda l, j: (l, 0, j))] + c_in,
        out_specs=[pl.BlockSpec((1, 8, tn), lambda l, j: (l, 0, j))] + c_out,
        out_shape=[jax.ShapeDtypeStruct((depth, 8, n), F32)] + c_shapes,
        compiler_params=_cparams("arbitrary", "arbitrary"),
        name="adaln_mod",
    )(c8, w_mod, b_mod.reshape(depth, 1, n), *c_args)


def _rope(x, c, s1, s2, shift):
    return x * c + pltpu.roll(x, HEAD_DIM - shift, 1) * s1 + pltpu.roll(x, shift, 1) * s2


def _inproj_kernel(*refs, use_rope, sub):
    x_ref, sh_ref, sc_ref, nrm_ref, w_ref, qn_ref, kn_ref = refs[:7]
    pos = 7
    rope_refs = ()
    if use_rope:
        rope_refs = refs[pos:pos + 6]
        pos += 6
    daq_ref, dak_ref, dav_ref, pool_ref, gq_ref, gk_ref, gv_ref = refs[pos:]

    def head_cols(i):
        return slice(i * HEAD_DIM, (i + 1) * HEAD_DIM)

    for r in range(x_ref.shape[0] // sub):
        rows = slice(r * sub, (r + 1) * sub)
        h = _modulate(x_ref[rows, :], nrm_ref[...], sh_ref[0], sc_ref[0]).astype(BF16)
        dac, das1, das2, gc, gs1, gs2 = (t[rows, :] for t in rope_refs) if use_rope else (None,) * 6

        def proj(off, width):
            return _dot(h, w_ref[:, off:off + width])

        a = proj(OFF_DAQ, DA_WIDTH)
        for i in range(DA_HEADS):
            t = a[:, head_cols(i)]
            if use_rope:
                t = _rope(t, dac, das1, das2, DA_HALF_DIM // 4)
            daq_ref[rows, head_cols(i)] = (t * (DA_HALF_DIM ** -0.5 * LOG2E)).astype(BF16)
        a = proj(OFF_DAK, DA_WIDTH)
        for i in range(DA_HEADS):
            t = a[:, head_cols(i)]
            if use_rope:
                t = _rope(t, dac, das1, das2, DA_HALF_DIM // 4)
            dak_ref[rows, head_cols(i)] = t.astype(BF16)
        dav_ref[rows, :] = proj(OFF_DAV, DA_WIDTH).astype(BF16)
        pool_ref[rows, :] = proj(OFF_POOL, POOL_WIDTH)
        a = proj(OFF_GQ, GQA_WIDTH)
        for i in range(GQA_Q_HEADS):
            t = _rms(a[:, head_cols(i)], qn_ref[...])
            if use_rope:
                t = _rope(t, gc, gs1, gs2, HEAD_DIM // 4)
            gq_ref[rows, head_cols(i)] = (t * (HEAD_DIM ** -0.5 * LOG2E)).astype(BF16)
        a = proj(OFF_GK, GQA_KV_WIDTH)
        for i in range(GQA_KV_HEADS):
            t = _rms(a[:, head_cols(i)], kn_ref[...])
            if use_rope:
                t = _rope(t, gc, gs1, gs2, HEAD_DIM // 4)
            gk_ref[rows, head_cols(i)] = t.astype(BF16)
        gv_ref[rows, :] = proj(OFF_GV, GQA_KV_WIDTH).astype(BF16)


def _mod_spec(ms, chunk, tpm):
    _, layer, fixed_row = ms

    def index(i, *_):
        r = i // tpm if fixed_row is None else fixed_row
        return ((layer * 8 + r) * N_MOD + chunk, 0, 0)

    return pl.BlockSpec((1, 1, D_MODEL), index)


def _inproj_call(x, ms, norm_gain, w_in_bf, q_norm, k_norm, rope, tm, rows_per_mod):
    m, d = x.shape
    tpm = rows_per_mod // tm
    row = lambda i: (i, 0)
    const = lambda i: (0, 0)
    shift, scale = ms[0], ms[0]
    in_specs = [pl.BlockSpec((tm, d), row),
                _mod_spec(ms, 0, tpm),
                _mod_spec(ms, 1, tpm),
                pl.BlockSpec((1, d), const),
                pl.BlockSpec((d, IN_COLS), const, pipeline_mode=pl.Buffered(1)),
                pl.BlockSpec((1, HEAD_DIM), const),
                pl.BlockSpec((1, HEAD_DIM), const)]
    args = [x, shift, scale, norm_gain.reshape(1, d), w_in_bf,
            q_norm.reshape(1, HEAD_DIM), k_norm.reshape(1, HEAD_DIM)]
    if rope is not None:
        tps = rope[0].shape[0] // tm
        in_specs += [pl.BlockSpec((tm, HEAD_DIM), lambda i: (i % tps, 0))] * 6
        args += list(rope)
    widths = (DA_WIDTH, DA_WIDTH, DA_WIDTH, POOL_WIDTH, GQA_WIDTH, GQA_KV_WIDTH, GQA_KV_WIDTH)
    dtypes = (BF16, BF16, BF16, F32, BF16, BF16, BF16)
    return pl.pallas_call(
        functools.partial(_inproj_kernel, use_rope=rope is not None, sub=min(tm, PROJ_SUB)),
        grid=(m // tm,),
        in_specs=in_specs,
        out_specs=[pl.BlockSpec((tm, w), row) for w in widths],
        out_shape=[jax.ShapeDtypeStruct((m, w), dt) for w, dt in zip(widths, dtypes)],
        compiler_params=_cparams("parallel"),
        name="in_proj",
    )(*args)


def _attn_kernel(*refs, nseg, tq, tiles_per_iter, lambda_init, n_cast):
    da = lambda_init is not None
    q_ref = refs[0]
    k_refs = refs[1:1 + nseg]
    v_refs = refs[1 + nseg:1 + 2 * nseg]
    pos = 1 + 2 * nseg
    if da:
        lq1, lk1, lq2, lk2, subln_ref = refs[pos:pos + 5]
        pos += 5
    cast_in = refs[pos:pos + n_cast]
    o_ref = refs[pos + n_cast]
    cast_out = refs[pos + n_cast + 1:pos + 2 * n_cast + 1]
    kcat, vcat = refs[pos + 2 * n_cast + 1:]
    _run_casts(cast_in, cast_out)

    n_kv = kcat.shape[0]
    n_q = q_ref.shape[1] // HEAD_DIM
    for kv in range(n_kv):
        cols = slice(kv * HEAD_DIM, (kv + 1) * HEAD_DIM)
        off = 0
        for kr, vr in zip(k_refs, v_refs):
            n = kr.shape[0]
            kcat[kv, off:off + n, :] = kr[:, cols]
            vcat[kv, off:off + n, :HEAD_DIM] = vr[:, cols]
            off += n
        vcat[kv, :, HEAD_DIM:] = jnp.ones((off, HEAD_DIM), BF16)

    def attend(q, kv):
        s = _dot_nt(q, kcat[kv])
        e = jnp.exp2(s - jnp.max(s, axis=-1, keepdims=True)).astype(BF16)
        o = _dot(e, vcat[kv])
        return o[:, :HEAD_DIM] / o[:, HEAD_DIM:]

    if da:
        lam = (jnp.exp(jnp.sum(lq1[...] * lk1[...], axis=-1, keepdims=True))
               - jnp.exp(jnp.sum(lq2[...] * lk2[...], axis=-1, keepdims=True)) + lambda_init)
        subln = subln_ref[...]

    def tile(t):
        rows = pl.ds(pl.multiple_of(t * tq, tq), tq)
        for g in range(n_q):
            cols = slice(g * HEAD_DIM, (g + 1) * HEAD_DIM)
            kv = g * n_kv // n_q
            q = q_ref[rows, cols]
            if da:
                lane = lax.broadcasted_iota(jnp.int32, q.shape, 1)
                zero = jnp.zeros_like(q)
                a1 = attend(jnp.where(lane < DA_HALF_DIM, q, zero), kv)
                a2 = attend(jnp.where(lane >= DA_HALF_DIM, q, zero), kv)
                o = _rms(a1 - lam * a2, subln) * (1.0 - lambda_init)
            else:
                o = attend(q, kv)
            o_ref[rows, cols] = o.astype(o_ref.dtype)

    n_iter = q_ref.shape[0] // (tq * tiles_per_iter)

    def body(it, carry):
        for u in range(tiles_per_iter):
            tile(it * tiles_per_iter + u)
        return carry

    lax.fori_loop(0, n_iter, body, 0)


def _attn_call(q, k_segs, v_segs, seg_lens, sq, tq, tiles_per_iter, n_heads, q_per_kv, kv_per_step,
               out_dtype, da_params=None, cast_jobs=()):
    mq = q.shape[0]
    nb = mq // sq
    nseg = len(k_segs)
    sk = sum(seg_lens)
    n_steps = n_heads // (q_per_kv * kv_per_step)
    c_in, c_out, c_shapes, c_args = _cast_specs(cast_jobs, lambda b, h: b * n_steps + h, nb * n_steps)
    qspec = pl.BlockSpec((sq, kv_per_step * q_per_kv * HEAD_DIM), lambda b, h: (b, h))
    kv_specs = [pl.BlockSpec((n, kv_per_step * HEAD_DIM), lambda b, h: (b, h)) for n in seg_lens]
    in_specs = [qspec] + kv_specs + kv_specs
    args = [q, *k_segs, *v_segs]
    lambda_init = None
    if da_params is not None:
        lam_vecs, subln, lambda_init = da_params
        vec = lambda n: pl.BlockSpec((1, n), lambda b, h: (0, 0))
        in_specs += [vec(DA_HALF_DIM)] * 4 + [vec(HEAD_DIM)]
        args += [v.reshape(1, DA_HALF_DIM) for v in lam_vecs] + [subln.reshape(1, HEAD_DIM)]
    return pl.pallas_call(
        functools.partial(_attn_kernel, nseg=nseg, tq=tq, tiles_per_iter=tiles_per_iter,
                          lambda_init=lambda_init, n_cast=len(cast_jobs)),
        grid=(nb, n_steps),
        in_specs=in_specs + c_in,
        out_specs=[qspec] + c_out,
        out_shape=[jax.ShapeDtypeStruct((mq, n_heads * HEAD_DIM), out_dtype)] + c_shapes,
        scratch_shapes=[pltpu.VMEM((kv_per_step, sk, HEAD_DIM), BF16),
                        pltpu.VMEM((kv_per_step, sk, 2 * HEAD_DIM), BF16)],
        compiler_params=_cparams("arbitrary", "arbitrary"),
        name="da_attn" if da_params is not None else "gqa_attn",
    )(*args, *c_args)


def _pool_kernel(u_ref, w_ref, ps_ref, o_ref):
    t = u_ref.shape[0]
    row = lax.broadcasted_iota(jnp.int32, (t, HEAD_DIM), 0)
    for g, w in enumerate(POOL_WINDOWS):
        sl = slice(g * HEAD_DIM, (g + 1) * HEAD_DIM)
        u = u_ref[:, sl]
        zeros = jnp.zeros((POOL_PAD, HEAD_DIM), F32)
        ue = jnp.concatenate([zeros, u, zeros], axis=0)
        n = t + 2 * POOL_PAD
        acc = ue + pltpu.roll(ue, 1, 0)
        m = 2
        while m < w:
            acc = pltpu.roll(acc, m // 2, 0) + pltpu.roll(acc, n - m // 2, 0)
            m *= 2
        cnt = jnp.minimum(row - w // 2 + w, t) - jnp.maximum(row - w // 2, 0)
        pooled = acc[POOL_PAD:POOL_PAD + t] / cnt.astype(F32) - u
        mixed = _dot(pooled.astype(BF16), w_ref[g].astype(BF16))
        o_ref[:, sl] = (mixed * ps_ref[:, sl]).astype(BF16)


def _pool_call(u, w_pool, pool_scale, seq):
    m = u.shape[0]
    return pl.pallas_call(
        _pool_kernel,
        grid=(m // seq,),
        in_specs=[pl.BlockSpec((seq, POOL_WIDTH), lambda b: (b, 0)),
                  pl.BlockSpec(w_pool.shape, lambda b: (0, 0, 0)),
                  pl.BlockSpec((1, POOL_WIDTH), lambda b: (0, 0))],
        out_specs=pl.BlockSpec((seq, POOL_WIDTH), lambda b: (b, 0)),
        out_shape=jax.ShapeDtypeStruct((m, POOL_WIDTH), BF16),
        compiler_params=_cparams("parallel"),
        name="pool_mixer",
    )(u, w_pool, pool_scale.reshape(1, POOL_WIDTH))


def _outproj_kernel(oda_ref, opool_ref, ogqa_ref, gn_ref, w_ref, x_ref, g1_ref,
                    nrm2_ref, sh2_ref, sc2_ref, xo_ref, h2_ref, *, sub):
    for r in range(x_ref.shape[0] // sub):
        rows = slice(r * sub, (r + 1) * sub)
        ogn = _rms(ogqa_ref[rows, :], gn_ref[...]).astype(BF16)
        cat = jnp.concatenate([oda_ref[rows, :], opool_ref[rows, :], ogn], axis=1)
        xn = x_ref[rows, :] + g1_ref[0] * _dot(cat, w_ref[...])
        xo_ref[rows, :] = xn
        h2_ref[rows, :] = _modulate(xn, nrm2_ref[...], sh2_ref[0], sc2_ref[0]).astype(BF16)


def _outproj_call(o_da, o_pool, o_gqa, gqa_out_norm, w_out_bf, x, ms, norm_ffn, tm, rows_per_mod):
    m, d = x.shape
    tpm = rows_per_mod // tm
    row = lambda i: (i, 0)
    const = lambda i: (0, 0)
    g1 = sh2 = sc2 = ms[0]
    return pl.pallas_call(
        functools.partial(_outproj_kernel, sub=min(tm, PROJ_SUB)),
        grid=(m // tm,),
        in_specs=[pl.BlockSpec((tm, DA_WIDTH), row),
                  pl.BlockSpec((tm, POOL_WIDTH), row),
                  pl.BlockSpec((tm, GQA_WIDTH), row),
                  pl.BlockSpec((1, GQA_WIDTH), const),
                  pl.BlockSpec((d, d), const, pipeline_mode=pl.Buffered(1)),
                  pl.BlockSpec((tm, d), row),
                  _mod_spec(ms, 2, tpm),
                  pl.BlockSpec((1, d), const),
                  _mod_spec(ms, 3, tpm), _mod_spec(ms, 4, tpm)],
        out_specs=[pl.BlockSpec((tm, d), row), pl.BlockSpec((tm, d), row)],
        out_shape=[jax.ShapeDtypeStruct((m, d), F32), jax.ShapeDtypeStruct((m, d), BF16)],
        compiler_params=_cparams("parallel"),
        name="out_proj",
    )(o_da, o_pool, o_gqa, gqa_out_norm.reshape(1, GQA_WIDTH), w_out_bf, x, g1,
      norm_ffn.reshape(1, d), sh2, sc2)


def _ffn_kernel(*refs, tm, tiles_per_seq, seq_starts, final):
    (h_ref, hp_ref, hn_ref, wg_ref, wv_ref, cwg_ref, cwv_ref, cbg_ref, cbv_ref, wd_ref,
     x_ref, g2_ref) = refs[:12]
    pos = 12
    if final:
        fn_ref = refs[pos]
        pos += 1
    o_ref, lhs_ref, uga_ref, ugb_ref, uva_ref, uvb_ref = refs[pos:]
    i = pl.program_id(0)
    j = pl.program_id(1)

    @pl.when(j == 0)
    def _():
        first = (i % tiles_per_seq) == 0
        last = (i % tiles_per_seq) == tiles_per_seq - 1
        zero = jnp.zeros_like(hp_ref[...])
        upper = lax.broadcasted_iota(jnp.int32, zero.shape, 0) < HALO // 2
        lhs_ref[0:tm, :] = h_ref[...]
        lhs_ref[tm:, :] = jnp.where(upper, jnp.where(last, zero, hn_ref[...]),
                                    jnp.where(first, zero, hp_ref[...]))
        o_ref[...] = jnp.zeros_like(o_ref)

    lhs = lhs_ref[...]

    def up(u_ref, w):
        r = _dot(lhs, w)
        u_ref[8:, :] = r
        u_ref[0:8, :] = r[tm + HALO - 8:]

    def conv(u_ref, cw, cb):
        def taps(start, n, fix=None):
            prev, nxt = u_ref[pl.ds(8 + start - 1, n), :], u_ref[pl.ds(8 + start + 1, n), :]
            if fix is not None:
                row = lax.broadcasted_iota(jnp.int32, prev.shape, 0) + start
                prev = jnp.where(row == fix, 0.0, prev)
                nxt = jnp.where(row == fix - 1, 0.0, nxt)
            return prev * cw[0:1] + u_ref[pl.ds(8 + start, n), :] * cw[1:2] + nxt * cw[2:3] + cb

        c = taps(0, tm)
        for b in seq_starts:
            c = jnp.concatenate([c[:b - 8], taps(b - 8, 16, fix=b), c[b + 8:]], axis=0)
        return c

    tf = wg_ref.shape[1]
    hf = tf // 2
    up(uga_ref, wg_ref[:, :hf])
    up(ugb_ref, wg_ref[:, hf:])
    up(uva_ref, wv_ref[:, :hf])
    up(uvb_ref, wv_ref[:, hf:])
    gate_a = _silu(conv(uga_ref, cwg_ref[:, :hf], cbg_ref[:, :hf]))
    gate_b = _silu(conv(ugb_ref, cwg_ref[:, hf:], cbg_ref[:, hf:]))
    act_a = (gate_a * conv(uva_ref, cwv_ref[:, :hf], cbv_ref[:, :hf])).astype(BF16)
    act_b = (gate_b * conv(uvb_ref, cwv_ref[:, hf:], cbv_ref[:, hf:])).astype(BF16)
    o_ref[...] += _dot(act_a, wd_ref[:hf, :]) + _dot(act_b, wd_ref[hf:, :])

    @pl.when(j == pl.num_programs(1) - 1)
    def _():
        out = x_ref[...] + g2_ref[0] * o_ref[...]
        if final:
            out = _rms(out, fn_ref[...])
        o_ref[...] = out


def _ffn_call(h2, w_up_bf, conv_w, conv_b, w_down_bf, x, ms, final_norm, tm, seq, rows_per_mod):
    m, d = x.shape
    tf = FFN_TF
    nj = D_FF // tf
    assert seq % tm == 0 or tm % seq == 0
    tps = max(seq // tm, 1)
    seq_starts = tuple(range(seq, tm, seq))
    tpm = rows_per_mod // tm
    hb = tm // HALO
    nhb = m // HALO
    final = final_norm is not None
    row = lambda i, j: (i, 0)
    gate_col = lambda i, j: (0, j)
    val_col = lambda i, j: (0, j + nj)
    in_specs = [pl.BlockSpec((tm, d), row),
                pl.BlockSpec((HALO, d), lambda i, j: (jnp.maximum(i * hb - 1, 0), 0)),
                pl.BlockSpec((HALO, d), lambda i, j: (jnp.minimum((i + 1) * hb, nhb - 1), 0)),
                pl.BlockSpec((d, tf), gate_col),
                pl.BlockSpec((d, tf), val_col),
                pl.BlockSpec((CONV_W, tf), gate_col),
                pl.BlockSpec((CONV_W, tf), val_col),
                pl.BlockSpec((1, tf), gate_col),
                pl.BlockSpec((1, tf), val_col),
                pl.BlockSpec((tf, d), lambda i, j: (j, 0)),
                pl.BlockSpec((tm, d), row),
                _mod_spec(ms, 5, tpm)]
    cb = conv_b.reshape(1, 2 * D_FF)
    args = [h2, h2, h2, w_up_bf, w_up_bf, conv_w, conv_w, cb, cb, w_down_bf, x, ms[0]]
    if final:
        in_specs.append(pl.BlockSpec((1, d), lambda i, j: (0, 0)))
        args.append(final_norm.reshape(1, d))
    return pl.pallas_call(
        functools.partial(_ffn_kernel, tm=tm, tiles_per_seq=tps, seq_starts=seq_starts, final=final),
        grid=(m // tm, nj),
        in_specs=in_specs,
        out_specs=pl.BlockSpec((tm, d), row),
        out_shape=jax.ShapeDtypeStruct((m, d), F32),
        scratch_shapes=([pltpu.VMEM((tm + HALO, d), BF16)]
                        + [pltpu.VMEM((tm + HALO + 8, tf // 2), F32)] * 4),
        compiler_params=_cparams("parallel", "arbitrary"),
        name="conv_ffn",
    )(*args)


def _rope_tables(seq, rot_dim):
    f32 = np.float32
    rows = seq // GRID_W
    row = np.repeat(np.arange(rows, dtype=f32), GRID_W)
    col = np.tile(np.arange(GRID_W, dtype=f32), rows)
    axis_dim = rot_dim // 2
    nf = axis_dim // 2
    freqs = f32(ROPE_BASE) ** (-(np.arange(nf, dtype=f32) * f32(2.0) / f32(axis_dim)))
    ar, ac = row[:, None] * freqs, col[:, None] * freqs
    cos = np.concatenate([np.cos(ar)] * 2 + [np.cos(ac)] * 2, axis=-1)
    sin = np.concatenate([np.sin(ar)] * 2 + [np.sin(ac)] * 2, axis=-1)
    first = np.tile(np.concatenate([np.ones((nf,), f32), np.zeros((nf,), f32)]), 2)
    reps = HEAD_DIM // rot_dim
    tables = (np.tile(cos, (1, reps)), np.tile(-sin * first, (1, reps)),
              np.tile(sin * (f32(1.0) - first), (1, reps)))
    return tuple(jnp.asarray(t, dtype=F32) for t in tables)


def kernel(x, c, ctx, c_ctx, w_mod, b_mod, norm_mix, norm_ffn, w_in, da_lambda_q1, da_lambda_k1,
           da_lambda_q2, da_lambda_k2, da_subln, gqa_q_norm, gqa_k_norm, pool_w, pool_scale,
           gqa_out_norm, w_out, w_up, conv_w, conv_b, w_down, final_norm):
    nb, seq, d = x.shape
    n_ctx = ctx.shape[1]
    depth = w_mod.shape[0]
    assert d == D_MODEL and seq % GRID_W == 0 and nb <= 7

    tm_lat = min(512, seq)
    tm_ctx = min(512, nb * n_ctx)
    tq_lat = min(128, seq)
    tq_ctx = min(256, n_ctx)
    tpi_lat = math.gcd(seq // tq_lat, 16)

    c8 = jnp.zeros((8, d), F32).at[:nb].set(c).at[nb].set(c_ctx)
    mods, w_in_bf = _mod_call(c8, w_mod, b_mod, [(w_in, 0)])
    mods_r = mods.reshape(depth * 8 * N_MOD, 1, d)

    rope = _rope_tables(seq, DA_HALF_DIM) + _rope_tables(seq, HEAD_DIM)

    xl = x.reshape(nb * seq, d)
    xc = ctx.reshape(nb * n_ctx, d)

    for l in range(depth):
        last = l == depth - 1
        lambda_init = 0.8 - 0.6 * math.exp(-0.3 * l)
        ms_lat = (mods_r, l, None)
        ms_ctx = (mods_r, l, nb)
        da_params = ((da_lambda_q1[l], da_lambda_k1[l], da_lambda_q2[l], da_lambda_k2[l]),
                     da_subln[l], lambda_init)

        pl_ = _inproj_call(xl, ms_lat, norm_mix[l], w_in_bf, gqa_q_norm[l], gqa_k_norm[l],
                           rope, tm_lat, seq)
        pc_ = _inproj_call(xc, ms_ctx, norm_mix[l], w_in_bf, gqa_q_norm[l], gqa_k_norm[l],
                           None, tm_ctx, nb * n_ctx)
        daq_l, dak_l, dav_l, pool_l, gq_l, gk_l, gv_l = pl_
        daq_c, dak_c, dav_c, pool_c, gq_c, gk_c, gv_c = pc_

        o_da, w_up_bf, w_down_bf = _attn_call(
            daq_l, [dak_c, dak_l], [dav_c, dav_l], [n_ctx, seq], seq, tq_lat, tpi_lat,
            DA_HEADS, 1, 1, BF16, da_params, cast_jobs=[(w_up, l), (w_down, l)])
        o_gqa, w_out_bf, *w_in_next = _attn_call(
            gq_l, [gk_c, gk_l], [gv_c, gv_l], [n_ctx, seq], seq, tq_lat, tpi_lat,
            GQA_Q_HEADS, GQA_GROUP, 1, F32, cast_jobs=[(w_out, l)] + ([] if last else [(w_in, l + 1)]))
        ffn_w = (w_up_bf, conv_w[l], conv_b[l], w_down_bf)
        o_pool = _pool_call(pool_l, pool_w[l], pool_scale[l], seq)
        xl, h2 = _outproj_call(o_da, o_pool, o_gqa, gqa_out_norm[l], w_out_bf, xl, ms_lat,
                               norm_ffn[l], tm_lat, seq)
        xl = _ffn_call(h2, *ffn_w, xl, ms_lat, final_norm if last else None, tm_lat, seq, seq)
        if not last:
            o_da, = _attn_call(daq_c, [dak_c], [dav_c], [n_ctx], n_ctx, tq_ctx, 1,
                               DA_HEADS, 1, DA_HEADS, BF16, da_params)
            o_gqa, = _attn_call(gq_c, [gk_c], [gv_c], [n_ctx], n_ctx, tq_ctx, 1,
                                GQA_Q_HEADS, GQA_GROUP, GQA_KV_HEADS, F32)
            o_pool = _pool_call(pool_c, pool_w[l], pool_scale[l], n_ctx)
            xc, h2 = _outproj_call(o_da, o_pool, o_gqa, gqa_out_norm[l], w_out_bf, xc, ms_ctx,
                                   norm_ffn[l], tm_ctx, nb * n_ctx)
            xc = _ffn_call(h2, *ffn_w, xc, ms_ctx, None, tm_ctx, n_ctx, nb * n_ctx)
            w_in_bf, = w_in_next

    return xl.reshape(nb, seq, d)
```

```python
import functools
import math

import jax
import jax.numpy as jnp
import numpy as np
from jax import lax
from jax.experimental import pallas as pl
from jax.experimental.pallas import tpu as pltpu

F32 = jnp.float32
BF16 = jnp.bfloat16

D_MODEL = 2048
GRID_W = 64
EPS = 1e-6
ROPE_BASE = 10000.0
N_MOD = 6
LOG2E = math.log2(math.e)

DA_HEADS = 6
DA_HALF_DIM = 64
HEAD_DIM = 128
DA_WIDTH = DA_HEADS * HEAD_DIM
POOL_WINDOWS = (2, 4, 8, 16)
POOL_WIDTH = len(POOL_WINDOWS) * HEAD_DIM
GQA_Q_HEADS = 6
GQA_KV_HEADS = 2
GQA_GROUP = GQA_Q_HEADS // GQA_KV_HEADS
GQA_WIDTH = GQA_Q_HEADS * HEAD_DIM
GQA_KV_WIDTH = GQA_KV_HEADS * HEAD_DIM
IN_COLS = 3 * DA_WIDTH + POOL_WIDTH + GQA_WIDTH + 2 * GQA_KV_WIDTH
D_FF = 5632
CONV_W = 3

OFF_DAQ = 0
OFF_DAK = OFF_DAQ + DA_WIDTH
OFF_DAV = OFF_DAK + DA_WIDTH
OFF_POOL = OFF_DAV + DA_WIDTH
OFF_GQ = OFF_POOL + POOL_WIDTH
OFF_GK = OFF_GQ + GQA_WIDTH
OFF_GV = OFF_GK + GQA_KV_WIDTH

V7X_VMEM_LIMIT_BYTES = 56 * 1024 * 1024
HALO = 16
FFN_TF = 512
PROJ_SUB = 256
CAST_STEPS = 16
POOL_PAD = 16


def _cparams(*sem):
    return pltpu.CompilerParams(dimension_semantics=sem, vmem_limit_bytes=V7X_VMEM_LIMIT_BYTES)


def _dot(a, b):
    return jnp.dot(a, b, preferred_element_type=F32)


def _dot_nt(a, b):
    return lax.dot_general(a, b, (((1,), (1,)), ((), ())), preferred_element_type=F32)


def _rms(x, gain):
    return x * lax.rsqrt(jnp.mean(x * x, axis=-1, keepdims=True) + EPS) * gain


def _modulate(x, gain, shift, scale):
    return _rms(x, gain) * (1.0 + scale) + shift


def _silu(x):
    return x / (1.0 + jnp.exp2(x * -LOG2E))


def _cast_specs(jobs, step_of, grid_steps):
    in_specs, out_specs, out_shapes, args = [], [], [], []
    n = min(CAST_STEPS, 1 << (grid_steps.bit_length() - 1))
    for w, layer in jobs:
        _, r, c = w.shape
        rb = r // n
        assert rb * n == r and rb % HALO == 0
        blk = lambda *g: jnp.minimum(step_of(*g), n - 1)
        in_specs.append(pl.BlockSpec((None, rb, c), lambda *g, layer=layer, blk=blk: (layer, blk(*g), 0)))
        out_specs.append(pl.BlockSpec((rb, c), lambda *g, blk=blk: (blk(*g), 0)))
        out_shapes.append(jax.ShapeDtypeStruct((r, c), BF16))
        args.append(w)
    return in_specs, out_specs, out_shapes, args


def _run_casts(in_refs, out_refs):
    for src, dst in zip(in_refs, out_refs):
        dst[...] = src[...].astype(BF16)


def _mod_kernel(*refs, n_cast):
    c_ref, w_ref, b_ref = refs[:3]
    cast_in = refs[3:3 + n_cast]
    o_ref = refs[3 + n_cast]
    cast_out = refs[4 + n_cast:]
    s = _silu(c_ref[...]).astype(BF16)
    o_ref[0] = _dot(s, w_ref[0].astype(BF16)) + b_ref[0]
    _run_casts(cast_in, cast_out)


def _mod_call(c8, w_mod, b_mod, cast_jobs, tn=1024):
    depth, d, n = w_mod.shape
    nj = n // tn
    c_in, c_out, c_shapes, c_args = _cast_specs(cast_jobs, lambda l, j: l * nj + j, depth * nj)
    return pl.pallas_call(
        functools.partial(_mod_kernel, n_cast=len(cast_jobs)),
        grid=(depth, nj),
        in_specs=[pl.BlockSpec((8, d), lambda l, j: (0, 0)),
                  pl.BlockSpec((1, d, tn), lambda l, j: (l, 0, j)),
                  pl.BlockSpec((1, 1, tn), lambda l, j: (l, 0, j))] + c_in,
        out_specs=[pl.BlockSpec((1, 8, tn), lambda l, j: (l, 0, j))] + c_out,
        out_shape=[jax.ShapeDtypeStruct((depth, 8, n), F32)] + c_shapes,
        compiler_params=_cparams("arbitrary", "arbitrary"),
        name="adaln_mod",
    )(c8, w_mod, b_mod.reshape(depth, 1, n), *c_args)


def _rope(x, c, s1, s2, shift):
    return x * c + pltpu.roll(x, HEAD_DIM - shift, 1) * s1 + pltpu.roll(x, shift, 1) * s2


def _inproj_kernel(*refs, use_rope, sub):
    x_ref, sh_ref, sc_ref, nrm_ref, w_ref, qn_ref, kn_ref = refs[:7]
    pos = 7
    rope_refs = ()
    if use_rope:
        rope_refs = refs[pos:pos + 6]
        pos += 6
    daq_ref, dak_ref, dav_ref, pool_ref, gq_ref, gk_ref, gv_ref = refs[pos:]

    def head_cols(i):
        return slice(i * HEAD_DIM, (i + 1) * HEAD_DIM)

    for r in range(x_ref.shape[0] // sub):
        rows = slice(r * sub, (r + 1) * sub)
        h = _modulate(x_ref[rows, :], nrm_ref[...], sh_ref[0], sc_ref[0]).astype(BF16)
        dac, das1, das2, gc, gs1, gs2 = (t[rows, :] for t in rope_refs) if use_rope else (None,) * 6

        def proj(off, width):
            return _dot(h, w_ref[:, off:off + width])

        a = proj(OFF_DAQ, DA_WIDTH)
        for i in range(DA_HEADS):
            t = a[:, head_cols(i)]
            if use_rope:
                t = _rope(t, dac, das1, das2, DA_HALF_DIM // 4)
            daq_ref[rows, head_cols(i)] = (t * (DA_HALF_DIM ** -0.5 * LOG2E)).astype(BF16)
        a = proj(OFF_DAK, DA_WIDTH)
        for i in range(DA_HEADS):
            t = a[:, head_cols(i)]
            if use_rope:
                t = _rope(t, dac, das1, das2, DA_HALF_DIM // 4)
            dak_ref[rows, head_cols(i)] = t.astype(BF16)
        dav_ref[rows, :] = proj(OFF_DAV, DA_WIDTH).astype(BF16)
        pool_ref[rows, :] = proj(OFF_POOL, POOL_WIDTH)
        a = proj(OFF_GQ, GQA_WIDTH)
        for i in range(GQA_Q_HEADS):
            t = _rms(a[:, head_cols(i)], qn_ref[...])
            if use_rope:
                t = _rope(t, gc, gs1, gs2, HEAD_DIM // 4)
            gq_ref[rows, head_cols(i)] = (t * (HEAD_DIM ** -0.5 * LOG2E)).astype(BF16)
        a = proj(OFF_GK, GQA_KV_WIDTH)
        for i in range(GQA_KV_HEADS):
            t = _rms(a[:, head_cols(i)], kn_ref[...])
            if use_rope:
                t = _rope(t, gc, gs1, gs2, HEAD_DIM // 4)
            gk_ref[rows, head_cols(i)] = t.astype(BF16)
        gv_ref[rows, :] = proj(OFF_GV, GQA_KV_WIDTH).astype(BF16)


def _mod_spec(ms, chunk, tpm):
    _, layer, fixed_row = ms

    def index(i, *_):
        r = i // tpm if fixed_row is None else fixed_row
        return ((layer * 8 + r) * N_MOD + chunk, 0, 0)

    return pl.BlockSpec((1, 1, D_MODEL), index)


def _inproj_call(x, ms, norm_gain, w_in_bf, q_norm, k_norm, rope, tm, rows_per_mod):
    m, d = x.shape
    tpm = rows_per_mod // tm
    row = lambda i: (i, 0)
    const = lambda i: (0, 0)
    shift, scale = ms[0], ms[0]
    in_specs = [pl.BlockSpec((tm, d), row),
                _mod_spec(ms, 0, tpm),
                _mod_spec(ms, 1, tpm),
                pl.BlockSpec((1, d), const),
                pl.BlockSpec((d, IN_COLS), const, pipeline_mode=pl.Buffered(1)),
                pl.BlockSpec((1, HEAD_DIM), const),
                pl.BlockSpec((1, HEAD_DIM), const)]
    args = [x, shift, scale, norm_gain.reshape(1, d), w_in_bf,
            q_norm.reshape(1, HEAD_DIM), k_norm.reshape(1, HEAD_DIM)]
    if rope is not None:
        tps = rope[0].shape[0] // tm
        in_specs += [pl.BlockSpec((tm, HEAD_DIM), lambda i: (i % tps, 0))] * 6
        args += list(rope)
    widths = (DA_WIDTH, DA_WIDTH, DA_WIDTH, POOL_WIDTH, GQA_WIDTH, GQA_KV_WIDTH, GQA_KV_WIDTH)
    dtypes = (BF16, BF16, BF16, F32, BF16, BF16, BF16)
    return pl.pallas_call(
        functools.partial(_inproj_kernel, use_rope=rope is not None, sub=min(tm, PROJ_SUB)),
        grid=(m // tm,),
        in_specs=in_specs,
        out_specs=[pl.BlockSpec((tm, w), row) for w in widths],
        out_shape=[jax.ShapeDtypeStruct((m, w), dt) for w, dt in zip(widths, dtypes)],
        compiler_params=_cparams("parallel"),
        name="in_proj",
    )(*args)


def _attn_kernel(*refs, nseg, tq, tiles_per_iter, lambda_init, n_cast):
    da = lambda_init is not None
    q_ref = refs[0]
    k_refs = refs[1:1 + nseg]
    v_refs = refs[1 + nseg:1 + 2 * nseg]
    pos = 1 + 2 * nseg
    if da:
        lq1, lk1, lq2, lk2, subln_ref = refs[pos:pos + 5]
        pos += 5
    cast_in = refs[pos:pos + n_cast]
    o_ref = refs[pos + n_cast]
    cast_out = refs[pos + n_cast + 1:pos + 2 * n_cast + 1]
    kcat, vcat = refs[pos + 2 * n_cast + 1:]
    _run_casts(cast_in, cast_out)

    n_kv = kcat.shape[0]
    n_q = q_ref.shape[1] // HEAD_DIM
    for kv in range(n_kv):
        cols = slice(kv * HEAD_DIM, (kv + 1) * HEAD_DIM)
        off = 0
        for kr, vr in zip(k_refs, v_refs):
            n = kr.shape[0]
            kcat[kv, off:off + n, :] = kr[:, cols]
            vcat[kv, off:off + n, :HEAD_DIM] = vr[:, cols]
            off += n
        vcat[kv, :, HEAD_DIM:] = jnp.ones((off, HEAD_DIM), BF16)

    def attend(q, kv):
        s = _dot_nt(q, kcat[kv])
        e = jnp.exp2(s - jnp.max(s, axis=-1, keepdims=True)).astype(BF16)
        o = _dot(e, vcat[kv])
        return o[:, :HEAD_DIM] / o[:, HEAD_DIM:]

    if da:
        lam = (jnp.exp(jnp.sum(lq1[...] * lk1[...], axis=-1, keepdims=True))
               - jnp.exp(jnp.sum(lq2[...] * lk2[...], axis=-1, keepdims=True)) + lambda_init)
        subln = subln_ref[...]

    def tile(t):
        rows = pl.ds(pl.multiple_of(t * tq, tq), tq)
        for g in range(n_q):
            cols = slice(g * HEAD_DIM, (g + 1) * HEAD_DIM)
            kv = g * n_kv // n_q
            q = q_ref[rows, cols]
            if da:
                lane = lax.broadcasted_iota(jnp.int32, q.shape, 1)
                zero = jnp.zeros_like(q)
                a1 = attend(jnp.where(lane < DA_HALF_DIM, q, zero), kv)
                a2 = attend(jnp.where(lane >= DA_HALF_DIM, q, zero), kv)
                o = _rms(a1 - lam * a2, subln) * (1.0 - lambda_init)
            else:
                o = attend(q, kv)
            o_ref[rows, cols] = o.astype(o_ref.dtype)

    n_iter = q_ref.shape[0] // (tq * tiles_per_iter)

    def body(it, carry):
        for u in range(tiles_per_iter):
            tile(it * tiles_per_iter + u)
        return carry

    lax.fori_loop(0, n_iter, body, 0)


def _attn_call(q, k_segs, v_segs, seg_lens, sq, tq, tiles_per_iter, n_heads, q_per_kv, kv_per_step,
               out_dtype, da_params=None, cast_jobs=()):
    mq = q.shape[0]
    nb = mq // sq
    nseg = len(k_segs)
    sk = sum(seg_lens)
    n_steps = n_heads // (q_per_kv * kv_per_step)
    c_in, c_out, c_shapes, c_args = _cast_specs(cast_jobs, lambda b, h: b * n_steps + h, nb * n_steps)
    qspec = pl.BlockSpec((sq, kv_per_step * q_per_kv * HEAD_DIM), lambda b, h: (b, h))
    kv_specs = [pl.BlockSpec((n, kv_per_step * HEAD_DIM), lambda b, h: (b, h)) for n in seg_lens]
    in_specs = [qspec] + kv_specs + kv_specs
    args = [q, *k_segs, *v_segs]
    lambda_init = None
    if da_params is not None:
        lam_vecs, subln, lambda_init = da_params
        vec = lambda n: pl.BlockSpec((1, n), lambda b, h: (0, 0))
        in_specs += [vec(DA_HALF_DIM)] * 4 + [vec(HEAD_DIM)]
        args += [v.reshape(1, DA_HALF_DIM) for v in lam_vecs] + [subln.reshape(1, HEAD_DIM)]
    return pl.pallas_call(
        functools.partial(_attn_kernel, nseg=nseg, tq=tq, tiles_per_iter=tiles_per_iter,
                          lambda_init=lambda_init, n_cast=len(cast_jobs)),
        grid=(nb, n_steps),
        in_specs=in_specs + c_in,
        out_specs=[qspec] + c_out,
        out_shape=[jax.ShapeDtypeStruct((mq, n_heads * HEAD_DIM), out_dtype)] + c_shapes,
        scratch_shapes=[pltpu.VMEM((kv_per_step, sk, HEAD_DIM), BF16),
                        pltpu.VMEM((kv_per_step, sk, 2 * HEAD_DIM), BF16)],
        compiler_params=_cparams("arbitrary", "arbitrary"),
        name="da_attn" if da_params is not None else "gqa_attn",
    )(*args, *c_args)


def _pool_kernel(u_ref, w_ref, ps_ref, o_ref):
    t = u_ref.shape[0]
    row = lax.broadcasted_iota(jnp.int32, (t, HEAD_DIM), 0)
    for g, w in enumerate(POOL_WINDOWS):
        sl = slice(g * HEAD_DIM, (g + 1) * HEAD_DIM)
        u = u_ref[:, sl]
        zeros = jnp.zeros((POOL_PAD, HEAD_DIM), F32)
        ue = jnp.concatenate([zeros, u, zeros], axis=0)
        n = t + 2 * POOL_PAD
        acc = ue + pltpu.roll(ue, 1, 0)
        m = 2
        while m < w:
            acc = pltpu.roll(acc, m // 2, 0) + pltpu.roll(acc, n - m // 2, 0)
            m *= 2
        cnt = jnp.minimum(row - w // 2 + w, t) - jnp.maximum(row - w // 2, 0)
        pooled = acc[POOL_PAD:POOL_PAD + t] / cnt.astype(F32) - u
        mixed = _dot(pooled.astype(BF16), w_ref[g].astype(BF16))
        o_ref[:, sl] = (mixed * ps_ref[:, sl]).astype(BF16)


def _pool_call(u, w_pool, pool_scale, seq):
    m = u.shape[0]
    return pl.pallas_call(
        _pool_kernel,
        grid=(m // seq,),
        in_specs=[pl.BlockSpec((seq, POOL_WIDTH), lambda b: (b, 0)),
                  pl.BlockSpec(w_pool.shape, lambda b: (0, 0, 0)),
                  pl.BlockSpec((1, POOL_WIDTH), lambda b: (0, 0))],
        out_specs=pl.BlockSpec((seq, POOL_WIDTH), lambda b: (b, 0)),
        out_shape=jax.ShapeDtypeStruct((m, POOL_WIDTH), BF16),
        compiler_params=_cparams("parallel"),
        name="pool_mixer",
    )(u, w_pool, pool_scale.reshape(1, POOL_WIDTH))


def _outproj_kernel(oda_ref, opool_ref, ogqa_ref, gn_ref, w_ref, x_ref, g1_ref,
                    nrm2_ref, sh2_ref, sc2_ref, xo_ref, h2_ref, *, sub):
    for r in range(x_ref.shape[0] // sub):
        rows = slice(r * sub, (r + 1) * sub)
        ogn = _rms(ogqa_ref[rows, :], gn_ref[...]).astype(BF16)
        cat = jnp.concatenate([oda_ref[rows, :], opool_ref[rows, :], ogn], axis=1)
        xn = x_ref[rows, :] + g1_ref[0] * _dot(cat, w_ref[...])
        xo_ref[rows, :] = xn
        h2_ref[rows, :] = _modulate(xn, nrm2_ref[...], sh2_ref[0], sc2_ref[0]).astype(BF16)


def _outproj_call(o_da, o_pool, o_gqa, gqa_out_norm, w_out_bf, x, ms, norm_ffn, tm, rows_per_mod):
    m, d = x.shape
    tpm = rows_per_mod // tm
    row = lambda i: (i, 0)
    const = lambda i: (0, 0)
    g1 = sh2 = sc2 = ms[0]
    return pl.pallas_call(
        functools.partial(_outproj_kernel, sub=min(tm, PROJ_SUB)),
        grid=(m // tm,),
        in_specs=[pl.BlockSpec((tm, DA_WIDTH), row),
                  pl.BlockSpec((tm, POOL_WIDTH), row),
                  pl.BlockSpec((tm, GQA_WIDTH), row),
                  pl.BlockSpec((1, GQA_WIDTH), const),
                  pl.BlockSpec((d, d), const, pipeline_mode=pl.Buffered(1)),
                  pl.BlockSpec((tm, d), row),
                  _mod_spec(ms, 2, tpm),
                  pl.BlockSpec((1, d), const),
                  _mod_spec(ms, 3, tpm), _mod_spec(ms, 4, tpm)],
        out_specs=[pl.BlockSpec((tm, d), row), pl.BlockSpec((tm, d), row)],
        out_shape=[jax.ShapeDtypeStruct((m, d), F32), jax.ShapeDtypeStruct((m, d), BF16)],
        compiler_params=_cparams("parallel"),
        name="out_proj",
    )(o_da, o_pool, o_gqa, gqa_out_norm.reshape(1, GQA_WIDTH), w_out_bf, x, g1,
      norm_ffn.reshape(1, d), sh2, sc2)


def _ffn_kernel(*refs, tm, tiles_per_seq, seq_starts, final):
    (h_ref, hp_ref, hn_ref, wg_ref, wv_ref, cwg_ref, cwv_ref, cbg_ref, cbv_ref, wd_ref,
     x_ref, g2_ref) = refs[:12]
    pos = 12
    if final:
        fn_ref = refs[pos]
        pos += 1
    o_ref, lhs_ref, uga_ref, ugb_ref, uva_ref, uvb_ref = refs[pos:]
    i = pl.program_id(0)
    j = pl.program_id(1)

    @pl.when(j == 0)
    def _():
        first = (i % tiles_per_seq) == 0
        last = (i % tiles_per_seq) == tiles_per_seq - 1
        zero = jnp.zeros_like(hp_ref[...])
        upper = lax.broadcasted_iota(jnp.int32, zero.shape, 0) < HALO // 2
        lhs_ref[0:tm, :] = h_ref[...]
        lhs_ref[tm:, :] = jnp.where(upper, jnp.where(last, zero, hn_ref[...]),
                                    jnp.where(first, zero, hp_ref[...]))
        o_ref[...] = jnp.zeros_like(o_ref)

    lhs = lhs_ref[...]

    def up(u_ref, w):
        r = _dot(lhs, w)
        u_ref[8:8 + tm + HALO, :] = r
        u_ref[0:8, :] = r[tm + HALO - 8:]

    def conv(u_ref, cw, cb):
        def taps(start, n, fix=None):
            prev, nxt = u_ref[pl.ds(8 + start - 1, n), :], u_ref[pl.ds(8 + start + 1, n), :]
            if fix is not None:
                row = lax.broadcasted_iota(jnp.int32, prev.shape, 0) + start
                prev = jnp.where(row == fix, 0.0, prev)
                nxt = jnp.where(row == fix - 1, 0.0, nxt)
            return prev * cw[0:1] + u_ref[pl.ds(8 + start, n), :] * cw[1:2] + nxt * cw[2:3] + cb

        c = taps(0, tm)
        for b in seq_starts:
            c = jnp.concatenate([c[:b - 8], taps(b - 8, 16, fix=b), c[b + 8:]], axis=0)
        return c

    tf = wg_ref.shape[1]
    hf = tf // 2
    up(uga_ref, wg_ref[:, :hf])
    up(ugb_ref, wg_ref[:, hf:])
    up(uva_ref, wv_ref[:, :hf])
    up(uvb_ref, wv_ref[:, hf:])
    gate_a = _silu(conv(uga_ref, cwg_ref[:, :hf], cbg_ref[:, :hf]))
    gate_b = _silu(conv(ugb_ref, cwg_ref[:, hf:], cbg_ref[:, hf:]))
    act_a = (gate_a * conv(uva_ref, cwv_ref[:, :hf], cbv_ref[:, :hf])).astype(BF16)
    act_b = (gate_b * conv(uvb_ref, cwv_ref[:, hf:], cbv_ref[:, hf:])).astype(BF16)
    o_ref[...] += _dot(act_a, wd_ref[:hf, :]) + _dot(act_b, wd_ref[hf:, :])

    @pl.when(j == pl.num_programs(1) - 1)
    def _():
        out = x_ref[...] + g2_ref[0] * o_ref[...]
        if final:
            out = _rms(out, fn_ref[...])
        o_ref[...] = out


def _ffn_call(h2, w_up_bf, conv_w, conv_b, w_down_bf, x, ms, final_norm, tm, seq, rows_per_mod):
    m, d = x.shape
    tf = FFN_TF
    nj = D_FF // tf
    assert seq % tm == 0 or tm % seq == 0
    tps = max(seq // tm, 1)
    seq_starts = tuple(range(seq, tm, seq))
    tpm = rows_per_mod // tm
    hb = tm // HALO
    nhb = m // HALO
    final = final_norm is not None
    row = lambda i, j: (i, 0)
    gate_col = lambda i, j: (0, j)
    val_col = lambda i, j: (0, j + nj)
    in_specs = [pl.BlockSpec((tm, d), row),
                pl.BlockSpec((HALO, d), lambda i, j: (jnp.maximum(i * hb - 1, 0), 0)),
                pl.BlockSpec((HALO, d), lambda i, j: (jnp.minimum((i + 1) * hb, nhb - 1), 0)),
                pl.BlockSpec((d, tf), gate_col),
                pl.BlockSpec((d, tf), val_col),
                pl.BlockSpec((CONV_W, tf), gate_col),
                pl.BlockSpec((CONV_W, tf), val_col),
                pl.BlockSpec((1, tf), gate_col),
                pl.BlockSpec((1, tf), val_col),
                pl.BlockSpec((tf, d), lambda i, j: (j, 0)),
                pl.BlockSpec((tm, d), row),
                _mod_spec(ms, 5, tpm)]
    cb = conv_b.reshape(1, 2 * D_FF)
    args = [h2, h2, h2, w_up_bf, w_up_bf, conv_w, conv_w, cb, cb, w_down_bf, x, ms[0]]
    if final:
        in_specs.append(pl.BlockSpec((1, d), lambda i, j: (0, 0)))
        args.append(final_norm.reshape(1, d))
    return pl.pallas_call(
        functools.partial(_ffn_kernel, tm=tm, tiles_per_seq=tps, seq_starts=seq_starts, final=final),
        grid=(m // tm, nj),
        in_specs=in_specs,
        out_specs=pl.BlockSpec((tm, d), row),
        out_shape=jax.ShapeDtypeStruct((m, d), F32),
        scratch_shapes=([pltpu.VMEM((tm + HALO, d), BF16)]
                        + [pltpu.VMEM((tm + 2 * HALO, tf // 2), F32)] * 4),
        compiler_params=_cparams("parallel", "arbitrary"),
        name="conv_ffn",
    )(*args)


def _rope_tables(seq, rot_dim):
    f32 = np.float32
    rows = seq // GRID_W
    row = np.repeat(np.arange(rows, dtype=f32), GRID_W)
    col = np.tile(np.arange(GRID_W, dtype=f32), rows)
    axis_dim = rot_dim // 2
    nf = axis_dim // 2
    freqs = f32(ROPE_BASE) ** (-(np.arange(nf, dtype=f32) * f32(2.0) / f32(axis_dim)))
    ar, ac = row[:, None] * freqs, col[:, None] * freqs
    cos = np.concatenate([np.cos(ar)] * 2 + [np.cos(ac)] * 2, axis=-1)
    sin = np.concatenate([np.sin(ar)] * 2 + [np.sin(ac)] * 2, axis=-1)
    first = np.tile(np.concatenate([np.ones((nf,), f32), np.zeros((nf,), f32)]), 2)
    reps = HEAD_DIM // rot_dim
    tables = (np.tile(cos, (1, reps)), np.tile(-sin * first, (1, reps)),
              np.tile(sin * (f32(1.0) - first), (1, reps)))
    return tuple(jnp.asarray(t, dtype=F32) for t in tables)


def kernel(x, c, ctx, c_ctx, w_mod, b_mod, norm_mix, norm_ffn, w_in, da_lambda_q1, da_lambda_k1,
           da_lambda_q2, da_lambda_k2, da_subln, gqa_q_norm, gqa_k_norm, pool_w, pool_scale,
           gqa_out_norm, w_out, w_up, conv_w, conv_b, w_down, final_norm):
    nb, seq, d = x.shape
    n_ctx = ctx.shape[1]
    depth = w_mod.shape[0]
    assert d == D_MODEL and seq % GRID_W == 0 and nb <= 7

    tm_lat = min(512, seq)
    tm_ctx = min(512, nb * n_ctx)
    tq_lat = min(128, seq)
    tq_ctx = min(256, n_ctx)
    tpi_lat = math.gcd(seq // tq_lat, 16)

    c8 = jnp.zeros((8, d), F32).at[:nb].set(c).at[nb].set(c_ctx)
    mods, w_in_bf = _mod_call(c8, w_mod, b_mod, [(w_in, 0)])
    mods_r = mods.reshape(depth * 8 * N_MOD, 1, d)

    rope = _rope_tables(seq, DA_HALF_DIM) + _rope_tables(seq, HEAD_DIM)

    xl = x.reshape(nb * seq, d)
    xc = ctx.reshape(nb * n_ctx, d)

    for l in range(depth):
        last = l == depth - 1
        lambda_init = 0.8 - 0.6 * math.exp(-0.3 * l)
        ms_lat = (mods_r, l, None)
        ms_ctx = (mods_r, l, nb)
        da_params = ((da_lambda_q1[l], da_lambda_k1[l], da_lambda_q2[l], da_lambda_k2[l]),
                     da_subln[l], lambda_init)

        pl_ = _inproj_call(xl, ms_lat, norm_mix[l], w_in_bf, gqa_q_norm[l], gqa_k_norm[l],
                           rope, tm_lat, seq)
        pc_ = _inproj_call(xc, ms_ctx, norm_mix[l], w_in_bf, gqa_q_norm[l], gqa_k_norm[l],
                           None, tm_ctx, nb * n_ctx)
        daq_l, dak_l, dav_l, pool_l, gq_l, gk_l, gv_l = pl_
        daq_c, dak_c, dav_c, pool_c, gq_c, gk_c, gv_c = pc_

        o_da, w_up_bf, w_down_bf = _attn_call(
            daq_l, [dak_c, dak_l], [dav_c, dav_l], [n_ctx, seq], seq, tq_lat, tpi_lat,
            DA_HEADS, 1, 1, BF16, da_params, cast_jobs=[(w_up, l), (w_down, l)])
        o_gqa, w_out_bf, *w_in_next = _attn_call(
            gq_l, [gk_c, gk_l], [gv_c, gv_l], [n_ctx, seq], seq, tq_lat, tpi_lat,
            GQA_Q_HEADS, GQA_GROUP, 1, F32, cast_jobs=[(w_out, l)] + ([] if last else [(w_in, l + 1)]))
        ffn_w = (w_up_bf, conv_w[l], conv_b[l], w_down_bf)
        o_pool = _pool_call(pool_l, pool_w[l], pool_scale[l], seq)
        xl, h2 = _outproj_call(o_da, o_pool, o_gqa, gqa_out_norm[l], w_out_bf, xl, ms_lat,
                               norm_ffn[l], tm_lat, seq)
        xl = _ffn_call(h2, *ffn_w, xl, ms_lat, final_norm if last else None, tm_lat, seq, seq)
        if not last:
            o_da, = _attn_call(daq_c, [dak_c], [dav_c], [n_ctx], n_ctx, tq_ctx, 1,
                               DA_HEADS, 1, DA_HEADS, BF16, da_params)
            o_gqa, = _attn_call(gq_c, [gk_c], [gv_c], [n_ctx], n_ctx, tq_ctx, 1,
                                GQA_Q_HEADS, GQA_GROUP, GQA_KV_HEADS, F32)
            o_pool = _pool_call(pool_c, pool_w[l], pool_scale[l], n_ctx)
            xc, h2 = _outproj_call(o_da, o_pool, o_gqa, gqa_out_norm[l], w_out_bf, xc, ms_ctx,
                                   norm_ffn[l], tm_ctx, nb * n_ctx)
            xc = _ffn_call(h2, *ffn_w, xc, ms_ctx, None, tm_ctx, n_ctx, nb * n_ctx)
            w_in_bf, = w_in_next

    return xl.reshape(nb, seq, d)
```

```python
import functools
import math

import jax
import jax.numpy as jnp
import numpy as np
from jax import lax
from jax.experimental import pallas as pl
from jax.experimental.pallas import tpu as pltpu

F32 = jnp.float32
BF16 = jnp.bfloat16

D_MODEL = 2048
GRID_W = 64
EPS = 1e-6
ROPE_BASE = 10000.0
N_MOD = 6
LOG2E = math.log2(math.e)

DA_HEADS = 6
DA_HALF_DIM = 64
HEAD_DIM = 128
DA_WIDTH = DA_HEADS * HEAD_DIM
POOL_WINDOWS = (2, 4, 8, 16)
POOL_WIDTH = len(POOL_WINDOWS) * HEAD_DIM
GQA_Q_HEADS = 6
GQA_KV_HEADS = 2
GQA_GROUP = GQA_Q_HEADS // GQA_KV_HEADS
GQA_WIDTH = GQA_Q_HEADS * HEAD_DIM
GQA_KV_WIDTH = GQA_KV_HEADS * HEAD_DIM
IN_COLS = 3 * DA_WIDTH + POOL_WIDTH + GQA_WIDTH + 2 * GQA_KV_WIDTH
D_FF = 5632
CONV_W = 3

OFF_DAQ = 0
OFF_DAK = OFF_DAQ + DA_WIDTH
OFF_DAV = OFF_DAK + DA_WIDTH
OFF_POOL = OFF_DAV + DA_WIDTH
OFF_GQ = OFF_POOL + POOL_WIDTH
OFF_GK = OFF_GQ + GQA_WIDTH
OFF_GV = OFF_GK + GQA_KV_WIDTH

V7X_VMEM_LIMIT_BYTES = 56 * 1024 * 1024
HALO = 16
FFN_TF = 512
PROJ_SUB = 256
CAST_STEPS = 16
POOL_PAD = 16


def _cparams(*sem):
    return pltpu.CompilerParams(dimension_semantics=sem, vmem_limit_bytes=V7X_VMEM_LIMIT_BYTES)


def _dot(a, b):
    return jnp.dot(a, b, preferred_element_type=F32)


def _dot_nt(a, b):
    return lax.dot_general(a, b, (((1,), (1,)), ((), ())), preferred_element_type=F32)


def _rms(x, gain):
    return x * lax.rsqrt(jnp.mean(x * x, axis=-1, keepdims=True) + EPS) * gain


def _modulate(x, gain, shift, scale):
    return _rms(x, gain) * (1.0 + scale) + shift


def _silu(x):
    return x / (1.0 + jnp.exp2(x * -LOG2E))


def _cast_specs(jobs, step_of, grid_steps):
    in_specs, out_specs, out_shapes, args = [], [], [], []
    n = min(CAST_STEPS, 1 << (grid_steps.bit_length() - 1))
    for w, layer in jobs:
        _, r, c = w.shape
        rb = r // n
        assert rb * n == r and rb % HALO == 0
        blk = lambda *g: jnp.minimum(step_of(*g), n - 1)
        in_specs.append(pl.BlockSpec((None, rb, c), lambda *g, layer=layer, blk=blk: (layer, blk(*g), 0)))
        out_specs.append(pl.BlockSpec((rb, c), lambda *g, blk=blk: (blk(*g), 0)))
        out_shapes.append(jax.ShapeDtypeStruct((r, c), BF16))
        args.append(w)
    return in_specs, out_specs, out_shapes, args


def _run_casts(in_refs, out_refs):
    for src, dst in zip(in_refs, out_refs):
        dst[...] = src[...].astype(BF16)


def _mod_kernel(*refs, n_cast):
    c_ref, w_ref, b_ref = refs[:3]
    cast_in = refs[3:3 + n_cast]
    o_ref = refs[3 + n_cast]
    cast_out = refs[4 + n_cast:]
    s = _silu(c_ref[...]).astype(BF16)
    o_ref[0] = _dot(s, w_ref[0].astype(BF16)) + b_ref[0]
    _run_casts(cast_in, cast_out)


def _mod_call(c8, w_mod, b_mod, cast_jobs, tn=1024):
    depth, d, n = w_mod.shape
    nj = n // tn
    c_in, c_out, c_shapes, c_args = _cast_specs(cast_jobs, lambda l, j: l * nj + j, depth * nj)
    return pl.pallas_call(
        functools.partial(_mod_kernel, n_cast=len(cast_jobs)),
        grid=(depth, nj),
        in_specs=[pl.BlockSpec((8, d), lambda l, j: (0, 0)),
                  pl.BlockSpec((1, d, tn), lambda l, j: (l, 0, j)),
                  pl.BlockSpec((1, 1, tn), lambda l, j: (l, 0, j))] + c_in,
        out_specs=[pl.BlockSpec((1, 8, tn), lambda l, j: (l, 0, j))] + c_out,
        out_shape=[jax.ShapeDtypeStruct((depth, 8, n), F32)] + c_shapes,
        compiler_params=_cparams("arbitrary", "arbitrary"),
        name="adaln_mod",
    )(c8, w_mod, b_mod.reshape(depth, 1, n), *c_args)


def _rope(x, c, s1, s2, shift):
    return x * c + pltpu.roll(x, HEAD_DIM - shift, 1) * s1 + pltpu.roll(x, shift, 1) * s2


def _inproj_kernel(*refs, use_rope, sub):
    x_ref, sh_ref, sc_ref, nrm_ref, w_ref, qn_ref, kn_ref = refs[:7]
    pos = 7
    rope_refs = ()
    if use_rope:
        rope_refs = refs[pos:pos + 6]
        pos += 6
    daq_ref, dak_ref, dav_ref, pool_ref, gq_ref, gk_ref, gv_ref = refs[pos:]

    def head_cols(i):
        return slice(i * HEAD_DIM, (i + 1) * HEAD_DIM)

    for r in range(x_ref.shape[0] // sub):
        rows = slice(r * sub, (r + 1) * sub)
        h = _modulate(x_ref[rows, :], nrm_ref[...], sh_ref[0], sc_ref[0]).astype(BF16)
        dac, das1, das2, gc, gs1, gs2 = (t[rows, :] for t in rope_refs) if use_rope else (None,) * 6

        def proj(off, width):
            return _dot(h, w_ref[:, off:off + width])

        a = proj(OFF_DAQ, DA_WIDTH)
        for i in range(DA_HEADS):
            t = a[:, head_cols(i)]
            if use_rope:
                t = _rope(t, dac, das1, das2, DA_HALF_DIM // 4)
            daq_ref[rows, head_cols(i)] = (t * (DA_HALF_DIM ** -0.5 * LOG2E)).astype(BF16)
        a = proj(OFF_DAK, DA_WIDTH)
        for i in range(DA_HEADS):
            t = a[:, head_cols(i)]
            if use_rope:
                t = _rope(t, dac, das1, das2, DA_HALF_DIM // 4)
            dak_ref[rows, head_cols(i)] = t.astype(BF16)
        dav_ref[rows, :] = proj(OFF_DAV, DA_WIDTH).astype(BF16)
        pool_ref[rows, :] = proj(OFF_POOL, POOL_WIDTH)
        a = proj(OFF_GQ, GQA_WIDTH)
        for i in range(GQA_Q_HEADS):
            t = _rms(a[:, head_cols(i)], qn_ref[...])
            if use_rope:
                t = _rope(t, gc, gs1, gs2, HEAD_DIM // 4)
            gq_ref[rows, head_cols(i)] = (t * (HEAD_DIM ** -0.5 * LOG2E)).astype(BF16)
        a = proj(OFF_GK, GQA_KV_WIDTH)
        for i in range(GQA_KV_HEADS):
            t = _rms(a[:, head_cols(i)], kn_ref[...])
            if use_rope:
                t = _rope(t, gc, gs1, gs2, HEAD_DIM // 4)
            gk_ref[rows, head_cols(i)] = t.astype(BF16)
        gv_ref[rows, :] = proj(OFF_GV, GQA_KV_WIDTH).astype(BF16)


def _mod_spec(ms, chunk, tpm):
    _, layer, fixed_row = ms

    def index(i, *_):
        r = i // tpm if fixed_row is None else fixed_row
        return ((layer * 8 + r) * N_MOD + chunk, 0, 0)

    return pl.BlockSpec((1, 1, D_MODEL), index)


def _inproj_call(x, ms, norm_gain, w_in_bf, q_norm, k_norm, rope, tm, rows_per_mod):
    m, d = x.shape
    tpm = rows_per_mod // tm
    row = lambda i: (i, 0)
    const = lambda i: (0, 0)
    shift, scale = ms[0], ms[0]
    in_specs = [pl.BlockSpec((tm, d), row),
                _mod_spec(ms, 0, tpm),
                _mod_spec(ms, 1, tpm),
                pl.BlockSpec((1, d), const),
                pl.BlockSpec((d, IN_COLS), const, pipeline_mode=pl.Buffered(1)),
                pl.BlockSpec((1, HEAD_DIM), const),
                pl.BlockSpec((1, HEAD_DIM), const)]
    args = [x, shift, scale, norm_gain.reshape(1, d), w_in_bf,
            q_norm.reshape(1, HEAD_DIM), k_norm.reshape(1, HEAD_DIM)]
    if rope is not None:
        tps = rope[0].shape[0] // tm
        in_specs += [pl.BlockSpec((tm, HEAD_DIM), lambda i: (i % tps, 0))] * 6
        args += list(rope)
    widths = (DA_WIDTH, DA_WIDTH, DA_WIDTH, POOL_WIDTH, GQA_WIDTH, GQA_KV_WIDTH, GQA_KV_WIDTH)
    dtypes = (BF16, BF16, BF16, F32, BF16, BF16, BF16)
    return pl.pallas_call(
        functools.partial(_inproj_kernel, use_rope=rope is not None, sub=min(tm, PROJ_SUB)),
        grid=(m // tm,),
        in_specs=in_specs,
        out_specs=[pl.BlockSpec((tm, w), row) for w in widths],
        out_shape=[jax.ShapeDtypeStruct((m, w), dt) for w, dt in zip(widths, dtypes)],
        compiler_params=_cparams("parallel"),
        name="in_proj",
    )(*args)


def _attn_kernel(*refs, nseg, tq, tiles_per_iter, lambda_init, n_cast):
    da = lambda_init is not None
    q_ref = refs[0]
    k_refs = refs[1:1 + nseg]
    v_refs = refs[1 + nseg:1 + 2 * nseg]
    pos = 1 + 2 * nseg
    if da:
        lq1, lk1, lq2, lk2, subln_ref = refs[pos:pos + 5]
        pos += 5
    cast_in = refs[pos:pos + n_cast]
    o_ref = refs[pos + n_cast]
    cast_out = refs[pos + n_cast + 1:pos + 2 * n_cast + 1]
    kcat, vcat = refs[pos + 2 * n_cast + 1:]
    _run_casts(cast_in, cast_out)

    n_kv = kcat.shape[0]
    n_q = q_ref.shape[1] // HEAD_DIM
    for kv in range(n_kv):
        cols = slice(kv * HEAD_DIM, (kv + 1) * HEAD_DIM)
        off = 0
        for kr, vr in zip(k_refs, v_refs):
            n = kr.shape[0]
            kcat[kv, off:off + n, :] = kr[:, cols]
            vcat[kv, off:off + n, :HEAD_DIM] = vr[:, cols]
            off += n
        vcat[kv, :, HEAD_DIM:] = jnp.ones((off, HEAD_DIM), BF16)

    def attend(q, kv):
        s = _dot_nt(q, kcat[kv])
        e = jnp.exp2(s - jnp.max(s, axis=-1, keepdims=True)).astype(BF16)
        o = _dot(e, vcat[kv])
        return o[:, :HEAD_DIM] / o[:, HEAD_DIM:]

    if da:
        lam = (jnp.exp(jnp.sum(lq1[...] * lk1[...], axis=-1, keepdims=True))
               - jnp.exp(jnp.sum(lq2[...] * lk2[...], axis=-1, keepdims=True)) + lambda_init)
        subln = subln_ref[...]

    def tile(t):
        rows = pl.ds(pl.multiple_of(t * tq, tq), tq)
        for g in range(n_q):
            cols = slice(g * HEAD_DIM, (g + 1) * HEAD_DIM)
            kv = g * n_kv // n_q
            q = q_ref[rows, cols]
            if da:
                lane = lax.broadcasted_iota(jnp.int32, q.shape, 1)
                zero = jnp.zeros_like(q)
                a1 = attend(jnp.where(lane < DA_HALF_DIM, q, zero), kv)
                a2 = attend(jnp.where(lane >= DA_HALF_DIM, q, zero), kv)
                o = _rms(a1 - lam * a2, subln) * (1.0 - lambda_init)
            else:
                o = attend(q, kv)
            o_ref[rows, cols] = o.astype(o_ref.dtype)

    n_iter = q_ref.shape[0] // (tq * tiles_per_iter)

    def body(it, carry):
        for u in range(tiles_per_iter):
            tile(it * tiles_per_iter + u)
        return carry

    lax.fori_loop(0, n_iter, body, 0)


def _attn_call(q, k_segs, v_segs, seg_lens, sq, tq, tiles_per_iter, n_heads, q_per_kv, kv_per_step,
               out_dtype, da_params=None, cast_jobs=()):
    mq = q.shape[0]
    nb = mq // sq
    nseg = len(k_segs)
    sk = sum(seg_lens)
    n_steps = n_heads // (q_per_kv * kv_per_step)
    c_in, c_out, c_shapes, c_args = _cast_specs(cast_jobs, lambda b, h: b * n_steps + h, nb * n_steps)
    qspec = pl.BlockSpec((sq, kv_per_step * q_per_kv * HEAD_DIM), lambda b, h: (b, h))
    kv_specs = [pl.BlockSpec((n, kv_per_step * HEAD_DIM), lambda b, h: (b, h)) for n in seg_lens]
    in_specs = [qspec] + kv_specs + kv_specs
    args = [q, *k_segs, *v_segs]
    lambda_init = None
    if da_params is not None:
        lam_vecs, subln, lambda_init = da_params
        vec = lambda n: pl.BlockSpec((1, n), lambda b, h: (0, 0))
        in_specs += [vec(DA_HALF_DIM)] * 4 + [vec(HEAD_DIM)]
        args += [v.reshape(1, DA_HALF_DIM) for v in lam_vecs] + [subln.reshape(1, HEAD_DIM)]
    return pl.pallas_call(
        functools.partial(_attn_kernel, nseg=nseg, tq=tq, tiles_per_iter=tiles_per_iter,
                          lambda_init=lambda_init, n_cast=len(cast_jobs)),
        grid=(nb, n_steps),
        in_specs=in_specs + c_in,
        out_specs=[qspec] + c_out,
        out_shape=[jax.ShapeDtypeStruct((mq, n_heads * HEAD_DIM), out_dtype)] + c_shapes,
        scratch_shapes=[pltpu.VMEM((kv_per_step, sk, HEAD_DIM), BF16),
                        pltpu.VMEM((kv_per_step, sk, 2 * HEAD_DIM), BF16)],
        compiler_params=_cparams("arbitrary", "arbitrary"),
        name="da_attn" if da_params is not None else "gqa_attn",
    )(*args, *c_args)


def _pool_kernel(u_ref, w_ref, ps_ref, o_ref):
    t = u_ref.shape[0]
    row = lax.broadcasted_iota(jnp.int32, (t, HEAD_DIM), 0)
    for g, w in enumerate(POOL_WINDOWS):
        sl = slice(g * HEAD_DIM, (g + 1) * HEAD_DIM)
        u = u_ref[:, sl]
        zeros = jnp.zeros((POOL_PAD, HEAD_DIM), F32)
        ue = jnp.concatenate([zeros, u, zeros], axis=0)
        n = t + 2 * POOL_PAD
        acc = ue + pltpu.roll(ue, 1, 0)
        m = 2
        while m < w:
            acc = pltpu.roll(acc, m // 2, 0) + pltpu.roll(acc, n - m // 2, 0)
            m *= 2
        cnt = jnp.minimum(row - w // 2 + w, t) - jnp.maximum(row - w // 2, 0)
        pooled = acc[POOL_PAD:POOL_PAD + t] / cnt.astype(F32) - u
        mixed = _dot(pooled.astype(BF16), w_ref[g].astype(BF16))
        o_ref[:, sl] = (mixed * ps_ref[:, sl]).astype(BF16)


def _pool_call(u, w_pool, pool_scale, seq):
    m = u.shape[0]
    return pl.pallas_call(
        _pool_kernel,
        grid=(m // seq,),
        in_specs=[pl.BlockSpec((seq, POOL_WIDTH), lambda b: (b, 0)),
                  pl.BlockSpec(w_pool.shape, lambda b: (0, 0, 0)),
                  pl.BlockSpec((1, POOL_WIDTH), lambda b: (0, 0))],
        out_specs=pl.BlockSpec((seq, POOL_WIDTH), lambda b: (b, 0)),
        out_shape=jax.ShapeDtypeStruct((m, POOL_WIDTH), BF16),
        compiler_params=_cparams("parallel"),
        name="pool_mixer",
    )(u, w_pool, pool_scale.reshape(1, POOL_WIDTH))


def _outproj_kernel(oda_ref, opool_ref, ogqa_ref, gn_ref, w_ref, x_ref, g1_ref,
                    nrm2_ref, sh2_ref, sc2_ref, xo_ref, h2_ref, *, sub):
    for r in range(x_ref.shape[0] // sub):
        rows = slice(r * sub, (r + 1) * sub)
        ogn = _rms(ogqa_ref[rows, :], gn_ref[...]).astype(BF16)
        cat = jnp.concatenate([oda_ref[rows, :], opool_ref[rows, :], ogn], axis=1)
        xn = x_ref[rows, :] + g1_ref[0] * _dot(cat, w_ref[...])
        xo_ref[rows, :] = xn
        h2_ref[rows, :] = _modulate(xn, nrm2_ref[...], sh2_ref[0], sc2_ref[0]).astype(BF16)


def _outproj_call(o_da, o_pool, o_gqa, gqa_out_norm, w_out_bf, x, ms, norm_ffn, tm, rows_per_mod):
    m, d = x.shape
    tpm = rows_per_mod // tm
    row = lambda i: (i, 0)
    const = lambda i: (0, 0)
    g1 = sh2 = sc2 = ms[0]
    return pl.pallas_call(
        functools.partial(_outproj_kernel, sub=min(tm, PROJ_SUB)),
        grid=(m // tm,),
        in_specs=[pl.BlockSpec((tm, DA_WIDTH), row),
                  pl.BlockSpec((tm, POOL_WIDTH), row),
                  pl.BlockSpec((tm, GQA_WIDTH), row),
                  pl.BlockSpec((1, GQA_WIDTH), const),
                  pl.BlockSpec((d, d), const, pipeline_mode=pl.Buffered(1)),
                  pl.BlockSpec((tm, d), row),
                  _mod_spec(ms, 2, tpm),
                  pl.BlockSpec((1, d), const),
                  _mod_spec(ms, 3, tpm), _mod_spec(ms, 4, tpm)],
        out_specs=[pl.BlockSpec((tm, d), row), pl.BlockSpec((tm, d), row)],
        out_shape=[jax.ShapeDtypeStruct((m, d), F32), jax.ShapeDtypeStruct((m, d), BF16)],
        compiler_params=_cparams("parallel"),
        name="out_proj",
    )(o_da, o_pool, o_gqa, gqa_out_norm.reshape(1, GQA_WIDTH), w_out_bf, x, g1,
      norm_ffn.reshape(1, d), sh2, sc2)


def _ffn_kernel(*refs, tm, tiles_per_seq, seq_starts, final):
    (h_ref, hp_ref, hn_ref, wg_ref, wv_ref, cwg_ref, cwv_ref, cbg_ref, cbv_ref, wd_ref,
     x_ref, g2_ref) = refs[:12]
    pos = 12
    if final:
        fn_ref = refs[pos]
        pos += 1
    o_ref, lhs_ref, uga_ref, ugb_ref, uva_ref, uvb_ref = refs[pos:]
    i = pl.program_id(0)
    j = pl.program_id(1)

    @pl.when(j == 0)
    def _():
        first = (i % tiles_per_seq) == 0
        last = (i % tiles_per_seq) == tiles_per_seq - 1
        lhs_ref[0:HALO, :] = jnp.where(first, jnp.zeros_like(hp_ref[...]), hp_ref[...])
        lhs_ref[HALO:HALO + tm, :] = h_ref[...]
        lhs_ref[HALO + tm:, :] = jnp.where(last, jnp.zeros_like(hn_ref[...]), hn_ref[...])
        o_ref[...] = jnp.zeros_like(o_ref)

    lhs = lhs_ref[...]

    def conv(u_ref, cw, cb):
        def taps(start, n, fix=None):
            prev, nxt = u_ref[pl.ds(HALO + start - 1, n), :], u_ref[pl.ds(HALO + start + 1, n), :]
            if fix is not None:
                row = lax.broadcasted_iota(jnp.int32, prev.shape, 0) + start
                prev = jnp.where(row == fix, 0.0, prev)
                nxt = jnp.where(row == fix - 1, 0.0, nxt)
            return prev * cw[0:1] + u_ref[pl.ds(HALO + start, n), :] * cw[1:2] + nxt * cw[2:3] + cb

        c = taps(0, tm)
        for b in seq_starts:
            c = jnp.concatenate([c[:b - 8], taps(b - 8, 16, fix=b), c[b + 8:]], axis=0)
        return c

    tf = wg_ref.shape[1]
    hf = tf // 2
    uga_ref[...] = _dot(lhs, wg_ref[:, :hf])
    ugb_ref[...] = _dot(lhs, wg_ref[:, hf:])
    uva_ref[...] = _dot(lhs, wv_ref[:, :hf])
    uvb_ref[...] = _dot(lhs, wv_ref[:, hf:])
    gate_a = _silu(conv(uga_ref, cwg_ref[:, :hf], cbg_ref[:, :hf]))
    gate_b = _silu(conv(ugb_ref, cwg_ref[:, hf:], cbg_ref[:, hf:]))
    act_a = (gate_a * conv(uva_ref, cwv_ref[:, :hf], cbv_ref[:, :hf])).astype(BF16)
    act_b = (gate_b * conv(uvb_ref, cwv_ref[:, hf:], cbv_ref[:, hf:])).astype(BF16)
    o_ref[...] += _dot(act_a, wd_ref[:hf, :]) + _dot(act_b, wd_ref[hf:, :])

    @pl.when(j == pl.num_programs(1) - 1)
    def _():
        out = x_ref[...] + g2_ref[0] * o_ref[...]
        if final:
            out = _rms(out, fn_ref[...])
        o_ref[...] = out


def _ffn_call(h2, w_up_bf, conv_w, conv_b, w_down_bf, x, ms, final_norm, tm, seq, rows_per_mod):
    m, d = x.shape
    tf = FFN_TF
    nj = D_FF // tf
    assert seq % tm == 0 or tm % seq == 0
    tps = max(seq // tm, 1)
    seq_starts = tuple(range(seq, tm, seq))
    tpm = rows_per_mod // tm
    hb = tm // HALO
    nhb = m // HALO
    final = final_norm is not None
    row = lambda i, j: (i, 0)
    gate_col = lambda i, j: (0, j)
    val_col = lambda i, j: (0, j + nj)
    in_specs = [pl.BlockSpec((tm, d), row),
                pl.BlockSpec((HALO, d), lambda i, j: (jnp.maximum(i * hb - 1, 0), 0)),
                pl.BlockSpec((HALO, d), lambda i, j: (jnp.minimum((i + 1) * hb, nhb - 1), 0)),
                pl.BlockSpec((d, tf), gate_col),
                pl.BlockSpec((d, tf), val_col),
                pl.BlockSpec((CONV_W, tf), gate_col),
                pl.BlockSpec((CONV_W, tf), val_col),
                pl.BlockSpec((1, tf), gate_col),
                pl.BlockSpec((1, tf), val_col),
                pl.BlockSpec((tf, d), lambda i, j: (j, 0)),
                pl.BlockSpec((tm, d), row),
                _mod_spec(ms, 5, tpm)]
    cb = conv_b.reshape(1, 2 * D_FF)
    args = [h2, h2, h2, w_up_bf, w_up_bf, conv_w, conv_w, cb, cb, w_down_bf, x, ms[0]]
    if final:
        in_specs.append(pl.BlockSpec((1, d), lambda i, j: (0, 0)))
        args.append(final_norm.reshape(1, d))
    return pl.pallas_call(
        functools.partial(_ffn_kernel, tm=tm, tiles_per_seq=tps, seq_starts=seq_starts, final=final),
        grid=(m // tm, nj),
        in_specs=in_specs,
        out_specs=pl.BlockSpec((tm, d), row),
        out_shape=jax.ShapeDtypeStruct((m, d), F32),
        scratch_shapes=([pltpu.VMEM((tm + 2 * HALO, d), BF16)]
                        + [pltpu.VMEM((tm + 2 * HALO, tf // 2), F32)] * 4),
        compiler_params=_cparams("parallel", "arbitrary"),
        name="conv_ffn",
    )(*args)


def _ffn_loop_kernel(*refs, tm, tiles_per_seq, final, n_tiles):
    h_ref, hp_ref, hn_ref, cw_ref, cb_ref, g2_ref = refs[:6]
    pos = 6
    if final:
        fn_ref = refs[pos]
        pos += 1
    x_hbm, wup_hbm, wdn_hbm, o_ref, lhs_ref = refs[pos:pos + 5]
    u_refs = refs[pos + 5:pos + 13]
    wg_buf, wv_buf, wd_buf, x_buf, sem, x_sem = refs[pos + 13:]
    i = pl.program_id(0)
    x_copy = pltpu.make_async_copy(x_hbm.at[pl.ds(pl.multiple_of(i * tm, tm), tm), :], x_buf, x_sem.at[0])
    x_copy.start()
    tf = FFN_TF
    hf = tf // 2
    nj = D_FF // tf
    n_pairs = nj // 2
    assert nj == 2 * n_pairs + 1 and n_pairs % 2 == 1

    def copies(j0, n, slot):
        c0 = j0 * tf if isinstance(j0, int) else pl.multiple_of(j0 * tf, tf)
        w = n * tf
        return (pltpu.make_async_copy(wup_hbm.at[:, pl.ds(c0, w)], wg_buf.at[slot, :, pl.ds(0, w)],
                                      sem.at[slot, 0]),
                pltpu.make_async_copy(wup_hbm.at[:, pl.ds(D_FF + c0, w)], wv_buf.at[slot, :, pl.ds(0, w)],
                                      sem.at[slot, 1]),
                pltpu.make_async_copy(wdn_hbm.at[pl.ds(c0, w), :], wd_buf.at[slot, pl.ds(0, w), :],
                                      sem.at[slot, 2]))

    def start(j0, n, slot):
        for c in copies(j0, n, slot):
            c.start()

    def wait(n, slot):
        for c in copies(0, n, slot):
            c.wait()

    @pl.when(i == 0)
    def _():
        start(0, 2, 0)

    first = (i % tiles_per_seq) == 0
    last = (i % tiles_per_seq) == tiles_per_seq - 1
    lhs_ref[0:HALO, :] = jnp.where(first, jnp.zeros_like(hp_ref[...]), hp_ref[...])
    lhs_ref[HALO:HALO + tm, :] = h_ref[...]
    lhs_ref[HALO + tm:, :] = jnp.where(last, jnp.zeros_like(hn_ref[...]), hn_ref[...])
    o_ref[...] = jnp.zeros_like(o_ref)

    def conv(u_ref, col):
        cols = pl.ds(col if isinstance(col, int) else pl.multiple_of(col, hf), hf)
        cw = cw_ref[:, cols]
        return (u_ref[pl.ds(HALO - 1, tm), :] * cw[0:1] + u_ref[pl.ds(HALO, tm), :] * cw[1:2]
                + u_ref[pl.ds(HALO + 1, tm), :] * cw[2:3] + cb_ref[:, cols])

    def step(j, slot, k, us):
        lhs = lhs_ref[...]
        wg, wv, wd = wg_buf.at[slot], wv_buf.at[slot], wd_buf.at[slot]
        c0 = k * tf
        uga, ugb, uva, uvb = us
        uga[...] = _dot(lhs, wg[:, c0:c0 + hf])
        ugb[...] = _dot(lhs, wg[:, c0 + hf:c0 + tf])
        uva[...] = _dot(lhs, wv[:, c0:c0 + hf])
        uvb[...] = _dot(lhs, wv[:, c0 + hf:c0 + tf])
        act_a = (_silu(conv(uga, j * tf)) * conv(uva, D_FF + j * tf)).astype(BF16)
        act_b = (_silu(conv(ugb, j * tf + hf)) * conv(uvb, D_FF + j * tf + hf)).astype(BF16)
        return _dot(act_a, wd[c0:c0 + hf, :]) + _dot(act_b, wd[c0 + hf:c0 + tf, :])

    def pair(p, carry):
        slot = p % 2
        wait(2, slot)

        @pl.when(p < n_pairs - 1)
        def _():
            start(2 * p + 2, 2, 1 - slot)

        @pl.when(p == n_pairs - 1)
        def _():
            start(nj - 1, 1, 1 - slot)

        o_ref[...] += step(2 * p, slot, 0, u_refs[:4]) + step(2 * p + 1, slot, 1, u_refs[4:])
        return carry

    lax.fori_loop(0, n_pairs, pair, 0)

    wait(1, 1)

    @pl.when(i < n_tiles - 1)
    def _():
        start(0, 2, 0)

    ffn = o_ref[...] + step(nj - 1, 1, 0, u_refs[:4])
    x_copy.wait()
    out = x_buf[...] + g2_ref[0] * ffn
    if final:
        out = _rms(out, fn_ref[...])
    o_ref[...] = out


def _ffn_loop_call(h2, w_up_bf, conv_w, conv_b, w_down_bf, x, ms, final_norm, tm, seq, rows_per_mod):
    m, d = x.shape
    tf = FFN_TF
    assert seq % tm == 0
    tps = seq // tm
    tpm = rows_per_mod // tm
    hb = tm // HALO
    nhb = m // HALO
    n_tiles = m // tm
    final = final_norm is not None
    row = lambda i: (i, 0)
    const = lambda i: (0, 0)
    in_specs = [pl.BlockSpec((tm, d), row),
                pl.BlockSpec((HALO, d), lambda i: (jnp.maximum(i * hb - 1, 0), 0)),
                pl.BlockSpec((HALO, d), lambda i: (jnp.minimum((i + 1) * hb, nhb - 1), 0)),
                pl.BlockSpec((CONV_W, 2 * D_FF), const),
                pl.BlockSpec((1, 2 * D_FF), const),
                _mod_spec(ms, 5, tpm)]
    args = [h2, h2, h2, conv_w, conv_b.reshape(1, 2 * D_FF), ms[0]]
    if final:
        in_specs.append(pl.BlockSpec((1, d), const))
        args.append(final_norm.reshape(1, d))
    in_specs += [pl.BlockSpec(memory_space=pl.ANY)] * 3
    args += [x, w_up_bf, w_down_bf]
    return pl.pallas_call(
        functools.partial(_ffn_loop_kernel, tm=tm, tiles_per_seq=tps, final=final, n_tiles=n_tiles),
        grid=(n_tiles,),
        in_specs=in_specs,
        out_specs=pl.BlockSpec((tm, d), row),
        out_shape=jax.ShapeDtypeStruct((m, d), F32),
        scratch_shapes=([pltpu.VMEM((tm + 2 * HALO, d), BF16)]
                        + [pltpu.VMEM((tm + 2 * HALO, tf // 2), F32)] * 8
                        + [pltpu.VMEM((2, d, 2 * tf), BF16), pltpu.VMEM((2, d, 2 * tf), BF16),
                           pltpu.VMEM((2, 2 * tf, d), BF16), pltpu.VMEM((tm, d), F32),
                           pltpu.SemaphoreType.DMA((2, 3)), pltpu.SemaphoreType.DMA((1,))]),
        compiler_params=pltpu.CompilerParams(dimension_semantics=("arbitrary",),
                                             vmem_limit_bytes=V7X_VMEM_LIMIT_BYTES + 4 * 1024 * 1024),
        name="conv_ffn_loop",
    )(*args)


def _rope_tables(seq, rot_dim):
    f32 = np.float32
    rows = seq // GRID_W
    row = np.repeat(np.arange(rows, dtype=f32), GRID_W)
    col = np.tile(np.arange(GRID_W, dtype=f32), rows)
    axis_dim = rot_dim // 2
    nf = axis_dim // 2
    freqs = f32(ROPE_BASE) ** (-(np.arange(nf, dtype=f32) * f32(2.0) / f32(axis_dim)))
    ar, ac = row[:, None] * freqs, col[:, None] * freqs
    cos = np.concatenate([np.cos(ar)] * 2 + [np.cos(ac)] * 2, axis=-1)
    sin = np.concatenate([np.sin(ar)] * 2 + [np.sin(ac)] * 2, axis=-1)
    first = np.tile(np.concatenate([np.ones((nf,), f32), np.zeros((nf,), f32)]), 2)
    reps = HEAD_DIM // rot_dim
    tables = (np.tile(cos, (1, reps)), np.tile(-sin * first, (1, reps)),
              np.tile(sin * (f32(1.0) - first), (1, reps)))
    return tuple(jnp.asarray(t, dtype=F32) for t in tables)


def kernel(x, c, ctx, c_ctx, w_mod, b_mod, norm_mix, norm_ffn, w_in, da_lambda_q1, da_lambda_k1,
           da_lambda_q2, da_lambda_k2, da_subln, gqa_q_norm, gqa_k_norm, pool_w, pool_scale,
           gqa_out_norm, w_out, w_up, conv_w, conv_b, w_down, final_norm):
    nb, seq, d = x.shape
    n_ctx = ctx.shape[1]
    depth = w_mod.shape[0]
    assert d == D_MODEL and seq % GRID_W == 0 and nb <= 7

    tm_lat = min(512, seq)
    tm_ctx = min(512, nb * n_ctx)
    tq_lat = min(128, seq)
    tq_ctx = min(256, n_ctx)
    tpi_lat = math.gcd(seq // tq_lat, 16)

    c8 = jnp.zeros((8, d), F32).at[:nb].set(c).at[nb].set(c_ctx)
    mods, w_in_bf = _mod_call(c8, w_mod, b_mod, [(w_in, 0)])
    mods_r = mods.reshape(depth * 8 * N_MOD, 1, d)

    rope = _rope_tables(seq, DA_HALF_DIM) + _rope_tables(seq, HEAD_DIM)

    xl = x.reshape(nb * seq, d)
    xc = ctx.reshape(nb * n_ctx, d)

    for l in range(depth):
        last = l == depth - 1
        lambda_init = 0.8 - 0.6 * math.exp(-0.3 * l)
        ms_lat = (mods_r, l, None)
        ms_ctx = (mods_r, l, nb)
        da_params = ((da_lambda_q1[l], da_lambda_k1[l], da_lambda_q2[l], da_lambda_k2[l]),
                     da_subln[l], lambda_init)

        pl_ = _inproj_call(xl, ms_lat, norm_mix[l], w_in_bf, gqa_q_norm[l], gqa_k_norm[l],
                           rope, tm_lat, seq)
        pc_ = _inproj_call(xc, ms_ctx, norm_mix[l], w_in_bf, gqa_q_norm[l], gqa_k_norm[l],
                           None, tm_ctx, nb * n_ctx)
        daq_l, dak_l, dav_l, pool_l, gq_l, gk_l, gv_l = pl_
        daq_c, dak_c, dav_c, pool_c, gq_c, gk_c, gv_c = pc_

        o_da, w_up_bf, w_down_bf = _attn_call(
            daq_l, [dak_c, dak_l], [dav_c, dav_l], [n_ctx, seq], seq, tq_lat, tpi_lat,
            DA_HEADS, 1, 1, BF16, da_params, cast_jobs=[(w_up, l), (w_down, l)])
        o_gqa, w_out_bf, *w_in_next = _attn_call(
            gq_l, [gk_c, gk_l], [gv_c, gv_l], [n_ctx, seq], seq, tq_lat, tpi_lat,
            GQA_Q_HEADS, GQA_GROUP, 1, F32, cast_jobs=[(w_out, l)] + ([] if last else [(w_in, l + 1)]))
        ffn_w = (w_up_bf, conv_w[l], conv_b[l], w_down_bf)
        o_pool = _pool_call(pool_l, pool_w[l], pool_scale[l], seq)
        xl, h2 = _outproj_call(o_da, o_pool, o_gqa, gqa_out_norm[l], w_out_bf, xl, ms_lat,
                               norm_ffn[l], tm_lat, seq)
        xl = _ffn_loop_call(h2, *ffn_w, xl, ms_lat, final_norm if last else None, tm_lat, seq, seq)
        if not last:
            o_da, = _attn_call(daq_c, [dak_c], [dav_c], [n_ctx], n_ctx, tq_ctx, 1,
                               DA_HEADS, 1, DA_HEADS, BF16, da_params)
            o_gqa, = _attn_call(gq_c, [gk_c], [gv_c], [n_ctx], n_ctx, tq_ctx, 1,
                                GQA_Q_HEADS, GQA_GROUP, GQA_KV_HEADS, F32)
            o_pool = _pool_call(pool_c, pool_w[l], pool_scale[l], n_ctx)
            xc, h2 = _outproj_call(o_da, o_pool, o_gqa, gqa_out_norm[l], w_out_bf, xc, ms_ctx,
                                   norm_ffn[l], tm_ctx, nb * n_ctx)
            xc = _ffn_call(h2, *ffn_w, xc, ms_ctx, None, tm_ctx, n_ctx, nb * n_ctx)
            w_in_bf, = w_in_next

    return xl.reshape(nb, seq, d)
```

```python
import functools
import math

import jax
import jax.numpy as jnp
import numpy as np
from jax import lax
from jax.experimental import pallas as pl
from jax.experimental.pallas import tpu as pltpu

F32 = jnp.float32
BF16 = jnp.bfloat16

D_MODEL = 2048
GRID_W = 64
EPS = 1e-6
ROPE_BASE = 10000.0
N_MOD = 6
LOG2E = math.log2(math.e)

DA_HEADS = 6
DA_HALF_DIM = 64
HEAD_DIM = 128
DA_WIDTH = DA_HEADS * HEAD_DIM
POOL_WINDOWS = (2, 4, 8, 16)
POOL_WIDTH = len(POOL_WINDOWS) * HEAD_DIM
GQA_Q_HEADS = 6
GQA_KV_HEADS = 2
GQA_GROUP = GQA_Q_HEADS // GQA_KV_HEADS
GQA_WIDTH = GQA_Q_HEADS * HEAD_DIM
GQA_KV_WIDTH = GQA_KV_HEADS * HEAD_DIM
IN_COLS = 3 * DA_WIDTH + POOL_WIDTH + GQA_WIDTH + 2 * GQA_KV_WIDTH
D_FF = 5632
CONV_W = 3

OFF_DAQ = 0
OFF_DAK = OFF_DAQ + DA_WIDTH
OFF_DAV = OFF_DAK + DA_WIDTH
OFF_POOL = OFF_DAV + DA_WIDTH
OFF_GQ = OFF_POOL + POOL_WIDTH
OFF_GK = OFF_GQ + GQA_WIDTH
OFF_GV = OFF_GK + GQA_KV_WIDTH

V7X_VMEM_LIMIT_BYTES = 56 * 1024 * 1024
HALO = 16
FFN_TF = 512
PROJ_SUB = 256
CAST_STEPS = 16
POOL_PAD = 16


def _cparams(*sem):
    return pltpu.CompilerParams(dimension_semantics=sem, vmem_limit_bytes=V7X_VMEM_LIMIT_BYTES)


def _dot(a, b):
    return jnp.dot(a, b, preferred_element_type=F32)


def _dot_nt(a, b):
    return lax.dot_general(a, b, (((1,), (1,)), ((), ())), preferred_element_type=F32)


def _rms(x, gain):
    return x * lax.rsqrt(jnp.mean(x * x, axis=-1, keepdims=True) + EPS) * gain


def _modulate(x, gain, shift, scale):
    return _rms(x, gain) * (1.0 + scale) + shift


def _silu(x):
    return x / (1.0 + jnp.exp2(x * -LOG2E))


def _cast_specs(jobs, step_of, grid_steps):
    in_specs, out_specs, out_shapes, args = [], [], [], []
    n = min(CAST_STEPS, 1 << (grid_steps.bit_length() - 1))
    for w, layer in jobs:
        _, r, c = w.shape
        rb = r // n
        assert rb * n == r and rb % HALO == 0
        blk = lambda *g: jnp.minimum(step_of(*g), n - 1)
        in_specs.append(pl.BlockSpec((None, rb, c), lambda *g, layer=layer, blk=blk: (layer, blk(*g), 0)))
        out_specs.append(pl.BlockSpec((rb, c), lambda *g, blk=blk: (blk(*g), 0)))
        out_shapes.append(jax.ShapeDtypeStruct((r, c), BF16))
        args.append(w)
    return in_specs, out_specs, out_shapes, args


def _run_casts(in_refs, out_refs):
    for src, dst in zip(in_refs, out_refs):
        dst[...] = src[...].astype(BF16)


def _mod_kernel(*refs, n_cast):
    c_ref, w_ref, b_ref = refs[:3]
    cast_in = refs[3:3 + n_cast]
    o_ref = refs[3 + n_cast]
    cast_out = refs[4 + n_cast:]
    s = _silu(c_ref[...]).astype(BF16)
    o_ref[0] = _dot(s, w_ref[0].astype(BF16)) + b_ref[0]
    _run_casts(cast_in, cast_out)


def _mod_call(c8, w_mod, b_mod, cast_jobs, tn=1024):
    depth, d, n = w_mod.shape
    nj = n // tn
    c_in, c_out, c_shapes, c_args = _cast_specs(cast_jobs, lambda l, j: l * nj + j, depth * nj)
    return pl.pallas_call(
        functools.partial(_mod_kernel, n_cast=len(cast_jobs)),
        grid=(depth, nj),
        in_specs=[pl.BlockSpec((8, d), lambda l, j: (0, 0)),
                  pl.BlockSpec((1, d, tn), lambda l, j: (l, 0, j)),
                  pl.BlockSpec((1, 1, tn), lambda l, j: (l, 0, j))] + c_in,
        out_specs=[pl.BlockSpec((1, 8, tn), lambda l, j: (l, 0, j))] + c_out,
        out_shape=[jax.ShapeDtypeStruct((depth, 8, n), F32)] + c_shapes,
        compiler_params=_cparams("arbitrary", "arbitrary"),
        name="adaln_mod",
    )(c8, w_mod, b_mod.reshape(depth, 1, n), *c_args)


def _rope(x, c, s1, s2, shift):
    return x * c + pltpu.roll(x, HEAD_DIM - shift, 1) * s1 + pltpu.roll(x, shift, 1) * s2


def _inproj_kernel(*refs, use_rope, sub):
    x_ref, sh_ref, sc_ref, nrm_ref, w_ref, qn_ref, kn_ref = refs[:7]
    pos = 7
    rope_refs = ()
    if use_rope:
        rope_refs = refs[pos:pos + 6]
        pos += 6
    daq_ref, dak_ref, dav_ref, pool_ref, gq_ref, gk_ref, gv_ref = refs[pos:]

    def head_cols(i):
        return slice(i * HEAD_DIM, (i + 1) * HEAD_DIM)

    for r in range(x_ref.shape[0] // sub):
        rows = slice(r * sub, (r + 1) * sub)
        h = _modulate(x_ref[rows, :], nrm_ref[...], sh_ref[0], sc_ref[0]).astype(BF16)
        dac, das1, das2, gc, gs1, gs2 = (t[rows, :] for t in rope_refs) if use_rope else (None,) * 6

        def proj(off, width):
            return _dot(h, w_ref[:, off:off + width])

        a = proj(OFF_DAQ, DA_WIDTH)
        for i in range(DA_HEADS):
            t = a[:, head_cols(i)]
            if use_rope:
                t = _rope(t, dac, das1, das2, DA_HALF_DIM // 4)
            daq_ref[rows, head_cols(i)] = (t * (DA_HALF_DIM ** -0.5 * LOG2E)).astype(BF16)
        a = proj(OFF_DAK, DA_WIDTH)
        for i in range(DA_HEADS):
            t = a[:, head_cols(i)]
            if use_rope:
                t = _rope(t, dac, das1, das2, DA_HALF_DIM // 4)
            dak_ref[rows, head_cols(i)] = t.astype(BF16)
        dav_ref[rows, :] = proj(OFF_DAV, DA_WIDTH).astype(BF16)
        pool_ref[rows, :] = proj(OFF_POOL, POOL_WIDTH)
        a = proj(OFF_GQ, GQA_WIDTH)
        for i in range(GQA_Q_HEADS):
            t = _rms(a[:, head_cols(i)], qn_ref[...])
            if use_rope:
                t = _rope(t, gc, gs1, gs2, HEAD_DIM // 4)
            gq_ref[rows, head_cols(i)] = (t * (HEAD_DIM ** -0.5 * LOG2E)).astype(BF16)
        a = proj(OFF_GK, GQA_KV_WIDTH)
        for i in range(GQA_KV_HEADS):
            t = _rms(a[:, head_cols(i)], kn_ref[...])
            if use_rope:
                t = _rope(t, gc, gs1, gs2, HEAD_DIM // 4)
            gk_ref[rows, head_cols(i)] = t.astype(BF16)
        gv_ref[rows, :] = proj(OFF_GV, GQA_KV_WIDTH).astype(BF16)


def _mod_spec(ms, chunk, tpm):
    _, layer, fixed_row = ms

    def index(i, *_):
        r = i // tpm if fixed_row is None else fixed_row
        return ((layer * 8 + r) * N_MOD + chunk, 0, 0)

    return pl.BlockSpec((1, 1, D_MODEL), index)


def _inproj_call(x, ms, norm_gain, w_in_bf, q_norm, k_norm, rope, tm, rows_per_mod):
    m, d = x.shape
    tpm = rows_per_mod // tm
    row = lambda i: (i, 0)
    const = lambda i: (0, 0)
    shift, scale = ms[0], ms[0]
    in_specs = [pl.BlockSpec((tm, d), row),
                _mod_spec(ms, 0, tpm),
                _mod_spec(ms, 1, tpm),
                pl.BlockSpec((1, d), const),
                pl.BlockSpec((d, IN_COLS), const, pipeline_mode=pl.Buffered(1)),
                pl.BlockSpec((1, HEAD_DIM), const),
                pl.BlockSpec((1, HEAD_DIM), const)]
    args = [x, shift, scale, norm_gain.reshape(1, d), w_in_bf,
            q_norm.reshape(1, HEAD_DIM), k_norm.reshape(1, HEAD_DIM)]
    if rope is not None:
        tps = rope[0].shape[0] // tm
        in_specs += [pl.BlockSpec((tm, HEAD_DIM), lambda i: (i % tps, 0))] * 6
        args += list(rope)
    widths = (DA_WIDTH, DA_WIDTH, DA_WIDTH, POOL_WIDTH, GQA_WIDTH, GQA_KV_WIDTH, GQA_KV_WIDTH)
    dtypes = (BF16, BF16, BF16, F32, BF16, BF16, BF16)
    return pl.pallas_call(
        functools.partial(_inproj_kernel, use_rope=rope is not None, sub=min(tm, PROJ_SUB)),
        grid=(m // tm,),
        in_specs=in_specs,
        out_specs=[pl.BlockSpec((tm, w), row) for w in widths],
        out_shape=[jax.ShapeDtypeStruct((m, w), dt) for w, dt in zip(widths, dtypes)],
        compiler_params=_cparams("parallel"),
        name="in_proj",
    )(*args)


def _attn_kernel(*refs, nseg, tq, tiles_per_iter, lambda_init, n_cast):
    da = lambda_init is not None
    q_ref = refs[0]
    k_refs = refs[1:1 + nseg]
    v_refs = refs[1 + nseg:1 + 2 * nseg]
    pos = 1 + 2 * nseg
    if da:
        lq1, lk1, lq2, lk2, subln_ref = refs[pos:pos + 5]
        pos += 5
    cast_in = refs[pos:pos + n_cast]
    o_ref = refs[pos + n_cast]
    cast_out = refs[pos + n_cast + 1:pos + 2 * n_cast + 1]
    kcat, vcat = refs[pos + 2 * n_cast + 1:]
    _run_casts(cast_in, cast_out)

    n_kv = kcat.shape[0]
    n_q = q_ref.shape[1] // HEAD_DIM
    for kv in range(n_kv):
        cols = slice(kv * HEAD_DIM, (kv + 1) * HEAD_DIM)
        off = 0
        for kr, vr in zip(k_refs, v_refs):
            n = kr.shape[0]
            kcat[kv, off:off + n, :] = kr[:, cols]
            vcat[kv, off:off + n, :HEAD_DIM] = vr[:, cols]
            off += n
        vcat[kv, :, HEAD_DIM:] = jnp.ones((off, HEAD_DIM), BF16)

    def attend(q, kv):
        s = _dot_nt(q, kcat[kv])
        e = jnp.exp2(s - jnp.max(s, axis=-1, keepdims=True)).astype(BF16)
        o = _dot(e, vcat[kv])
        return o[:, :HEAD_DIM] / o[:, HEAD_DIM:]

    if da:
        lam = (jnp.exp(jnp.sum(lq1[...] * lk1[...], axis=-1, keepdims=True))
               - jnp.exp(jnp.sum(lq2[...] * lk2[...], axis=-1, keepdims=True)) + lambda_init)
        subln = subln_ref[...]

    def tile(t):
        rows = pl.ds(pl.multiple_of(t * tq, tq), tq)
        for g in range(n_q):
            cols = slice(g * HEAD_DIM, (g + 1) * HEAD_DIM)
            kv = g * n_kv // n_q
            q = q_ref[rows, cols]
            if da:
                lane = lax.broadcasted_iota(jnp.int32, q.shape, 1)
                zero = jnp.zeros_like(q)
                a1 = attend(jnp.where(lane < DA_HALF_DIM, q, zero), kv)
                a2 = attend(jnp.where(lane >= DA_HALF_DIM, q, zero), kv)
                o = _rms(a1 - lam * a2, subln) * (1.0 - lambda_init)
            else:
                o = attend(q, kv)
            o_ref[rows, cols] = o.astype(o_ref.dtype)

    n_iter = q_ref.shape[0] // (tq * tiles_per_iter)

    def body(it, carry):
        for u in range(tiles_per_iter):
            tile(it * tiles_per_iter + u)
        return carry

    lax.fori_loop(0, n_iter, body, 0)


def _attn_call(q, k_segs, v_segs, seg_lens, sq, tq, tiles_per_iter, n_heads, q_per_kv, kv_per_step,
               out_dtype, da_params=None, cast_jobs=()):
    mq = q.shape[0]
    nb = mq // sq
    nseg = len(k_segs)
    sk = sum(seg_lens)
    n_steps = n_heads // (q_per_kv * kv_per_step)
    c_in, c_out, c_shapes, c_args = _cast_specs(cast_jobs, lambda b, h: b * n_steps + h, nb * n_steps)
    qspec = pl.BlockSpec((sq, kv_per_step * q_per_kv * HEAD_DIM), lambda b, h: (b, h))
    kv_specs = [pl.BlockSpec((n, kv_per_step * HEAD_DIM), lambda b, h: (b, h)) for n in seg_lens]
    in_specs = [qspec] + kv_specs + kv_specs
    args = [q, *k_segs, *v_segs]
    lambda_init = None
    if da_params is not None:
        lam_vecs, subln, lambda_init = da_params
        vec = lambda n: pl.BlockSpec((1, n), lambda b, h: (0, 0))
        in_specs += [vec(DA_HALF_DIM)] * 4 + [vec(HEAD_DIM)]
        args += [v.reshape(1, DA_HALF_DIM) for v in lam_vecs] + [subln.reshape(1, HEAD_DIM)]
    return pl.pallas_call(
        functools.partial(_attn_kernel, nseg=nseg, tq=tq, tiles_per_iter=tiles_per_iter,
                          lambda_init=lambda_init, n_cast=len(cast_jobs)),
        grid=(nb, n_steps),
        in_specs=in_specs + c_in,
        out_specs=[qspec] + c_out,
        out_shape=[jax.ShapeDtypeStruct((mq, n_heads * HEAD_DIM), out_dtype)] + c_shapes,
        scratch_shapes=[pltpu.VMEM((kv_per_step, sk, HEAD_DIM), BF16),
                        pltpu.VMEM((kv_per_step, sk, 2 * HEAD_DIM), BF16)],
        compiler_params=_cparams("arbitrary", "arbitrary"),
        name="da_attn" if da_params is not None else "gqa_attn",
    )(*args, *c_args)


def _pool_kernel(u_ref, w_ref, ps_ref, o_ref):
    t = u_ref.shape[0]
    row = lax.broadcasted_iota(jnp.int32, (t, HEAD_DIM), 0)
    for g, w in enumerate(POOL_WINDOWS):
        sl = slice(g * HEAD_DIM, (g + 1) * HEAD_DIM)
        u = u_ref[:, sl]
        zeros = jnp.zeros((POOL_PAD, HEAD_DIM), F32)
        ue = jnp.concatenate([zeros, u, zeros], axis=0)
        n = t + 2 * POOL_PAD
        acc = ue + pltpu.roll(ue, 1, 0)
        m = 2
        while m < w:
            acc = pltpu.roll(acc, m // 2, 0) + pltpu.roll(acc, n - m // 2, 0)
            m *= 2
        cnt = jnp.minimum(row - w // 2 + w, t) - jnp.maximum(row - w // 2, 0)
        pooled = acc[POOL_PAD:POOL_PAD + t] / cnt.astype(F32) - u
        mixed = _dot(pooled.astype(BF16), w_ref[g].astype(BF16))
        o_ref[:, sl] = (mixed * ps_ref[:, sl]).astype(BF16)


def _pool_call(u, w_pool, pool_scale, seq):
    m = u.shape[0]
    return pl.pallas_call(
        _pool_kernel,
        grid=(m // seq,),
        in_specs=[pl.BlockSpec((seq, POOL_WIDTH), lambda b: (b, 0)),
                  pl.BlockSpec(w_pool.shape, lambda b: (0, 0, 0)),
                  pl.BlockSpec((1, POOL_WIDTH), lambda b: (0, 0))],
        out_specs=pl.BlockSpec((seq, POOL_WIDTH), lambda b: (b, 0)),
        out_shape=jax.ShapeDtypeStruct((m, POOL_WIDTH), BF16),
        compiler_params=_cparams("parallel"),
        name="pool_mixer",
    )(u, w_pool, pool_scale.reshape(1, POOL_WIDTH))


def _outproj_kernel(oda_ref, opool_ref, ogqa_ref, gn_ref, w_ref, x_ref, g1_ref,
                    nrm2_ref, sh2_ref, sc2_ref, xo_ref, h2_ref, *, sub):
    for r in range(x_ref.shape[0] // sub):
        rows = slice(r * sub, (r + 1) * sub)
        ogn = _rms(ogqa_ref[rows, :], gn_ref[...]).astype(BF16)
        cat = jnp.concatenate([oda_ref[rows, :], opool_ref[rows, :], ogn], axis=1)
        xn = x_ref[rows, :] + g1_ref[0] * _dot(cat, w_ref[...])
        xo_ref[rows, :] = xn
        h2_ref[rows, :] = _modulate(xn, nrm2_ref[...], sh2_ref[0], sc2_ref[0]).astype(BF16)


def _outproj_call(o_da, o_pool, o_gqa, gqa_out_norm, w_out_bf, x, ms, norm_ffn, tm, rows_per_mod):
    m, d = x.shape
    tpm = rows_per_mod // tm
    row = lambda i: (i, 0)
    const = lambda i: (0, 0)
    g1 = sh2 = sc2 = ms[0]
    return pl.pallas_call(
        functools.partial(_outproj_kernel, sub=min(tm, PROJ_SUB)),
        grid=(m // tm,),
        in_specs=[pl.BlockSpec((tm, DA_WIDTH), row),
                  pl.BlockSpec((tm, POOL_WIDTH), row),
                  pl.BlockSpec((tm, GQA_WIDTH), row),
                  pl.BlockSpec((1, GQA_WIDTH), const),
                  pl.BlockSpec((d, d), const, pipeline_mode=pl.Buffered(1)),
                  pl.BlockSpec((tm, d), row),
                  _mod_spec(ms, 2, tpm),
                  pl.BlockSpec((1, d), const),
                  _mod_spec(ms, 3, tpm), _mod_spec(ms, 4, tpm)],
        out_specs=[pl.BlockSpec((tm, d), row), pl.BlockSpec((tm, d), row)],
        out_shape=[jax.ShapeDtypeStruct((m, d), F32), jax.ShapeDtypeStruct((m, d), BF16)],
        compiler_params=_cparams("parallel"),
        name="out_proj",
    )(o_da, o_pool, o_gqa, gqa_out_norm.reshape(1, GQA_WIDTH), w_out_bf, x, g1,
      norm_ffn.reshape(1, d), sh2, sc2)


def _ffn_kernel(*refs, tm, tiles_per_seq, seq_starts, final):
    (h_ref, hp_ref, hn_ref, wg_ref, wv_ref, cwg_ref, cwv_ref, cbg_ref, cbv_ref, wd_ref,
     x_ref, g2_ref) = refs[:12]
    pos = 12
    if final:
        fn_ref = refs[pos]
        pos += 1
    o_ref, lhs_ref, uga_ref, ugb_ref, uva_ref, uvb_ref = refs[pos:]
    i = pl.program_id(0)
    j = pl.program_id(1)

    @pl.when(j == 0)
    def _():
        first = (i % tiles_per_seq) == 0
        last = (i % tiles_per_seq) == tiles_per_seq - 1
        lhs_ref[0:HALO, :] = jnp.where(first, jnp.zeros_like(hp_ref[...]), hp_ref[...])
        lhs_ref[HALO:HALO + tm, :] = h_ref[...]
        lhs_ref[HALO + tm:, :] = jnp.where(last, jnp.zeros_like(hn_ref[...]), hn_ref[...])
        o_ref[...] = jnp.zeros_like(o_ref)

    lhs = lhs_ref[...]

    def conv(u_ref, cw, cb):
        def taps(start, n, fix=None):
            prev, nxt = u_ref[pl.ds(HALO + start - 1, n), :], u_ref[pl.ds(HALO + start + 1, n), :]
            if fix is not None:
                row = lax.broadcasted_iota(jnp.int32, prev.shape, 0) + start
                prev = jnp.where(row == fix, 0.0, prev)
                nxt = jnp.where(row == fix - 1, 0.0, nxt)
            return prev * cw[0:1] + u_ref[pl.ds(HALO + start, n), :] * cw[1:2] + nxt * cw[2:3] + cb

        c = taps(0, tm)
        for b in seq_starts:
            c = jnp.concatenate([c[:b - 8], taps(b - 8, 16, fix=b), c[b + 8:]], axis=0)
        return c

    tf = wg_ref.shape[1]
    hf = tf // 2
    uga_ref[...] = _dot(lhs, wg_ref[:, :hf])
    ugb_ref[...] = _dot(lhs, wg_ref[:, hf:])
    uva_ref[...] = _dot(lhs, wv_ref[:, :hf])
    uvb_ref[...] = _dot(lhs, wv_ref[:, hf:])
    gate_a = _silu(conv(uga_ref, cwg_ref[:, :hf], cbg_ref[:, :hf]))
    gate_b = _silu(conv(ugb_ref, cwg_ref[:, hf:], cbg_ref[:, hf:]))
    act_a = (gate_a * conv(uva_ref, cwv_ref[:, :hf], cbv_ref[:, :hf])).astype(BF16)
    act_b = (gate_b * conv(uvb_ref, cwv_ref[:, hf:], cbv_ref[:, hf:])).astype(BF16)
    o_ref[...] += _dot(act_a, wd_ref[:hf, :]) + _dot(act_b, wd_ref[hf:, :])

    @pl.when(j == pl.num_programs(1) - 1)
    def _():
        out = x_ref[...] + g2_ref[0] * o_ref[...]
        if final:
            out = _rms(out, fn_ref[...])
        o_ref[...] = out


def _ffn_call(h2, w_up_bf, conv_w, conv_b, w_down_bf, x, ms, final_norm, tm, seq, rows_per_mod):
    m, d = x.shape
    tf = FFN_TF
    nj = D_FF // tf
    assert seq % tm == 0 or tm % seq == 0
    tps = max(seq // tm, 1)
    seq_starts = tuple(range(seq, tm, seq))
    tpm = rows_per_mod // tm
    hb = tm // HALO
    nhb = m // HALO
    final = final_norm is not None
    row = lambda i, j: (i, 0)
    gate_col = lambda i, j: (0, j)
    val_col = lambda i, j: (0, j + nj)
    in_specs = [pl.BlockSpec((tm, d), row),
                pl.BlockSpec((HALO, d), lambda i, j: (jnp.maximum(i * hb - 1, 0), 0)),
                pl.BlockSpec((HALO, d), lambda i, j: (jnp.minimum((i + 1) * hb, nhb - 1), 0)),
                pl.BlockSpec((d, tf), gate_col),
                pl.BlockSpec((d, tf), val_col),
                pl.BlockSpec((CONV_W, tf), gate_col),
                pl.BlockSpec((CONV_W, tf), val_col),
                pl.BlockSpec((1, tf), gate_col),
                pl.BlockSpec((1, tf), val_col),
                pl.BlockSpec((tf, d), lambda i, j: (j, 0)),
                pl.BlockSpec((tm, d), row),
                _mod_spec(ms, 5, tpm)]
    cb = conv_b.reshape(1, 2 * D_FF)
    args = [h2, h2, h2, w_up_bf, w_up_bf, conv_w, conv_w, cb, cb, w_down_bf, x, ms[0]]
    if final:
        in_specs.append(pl.BlockSpec((1, d), lambda i, j: (0, 0)))
        args.append(final_norm.reshape(1, d))
    return pl.pallas_call(
        functools.partial(_ffn_kernel, tm=tm, tiles_per_seq=tps, seq_starts=seq_starts, final=final),
        grid=(m // tm, nj),
        in_specs=in_specs,
        out_specs=pl.BlockSpec((tm, d), row),
        out_shape=jax.ShapeDtypeStruct((m, d), F32),
        scratch_shapes=([pltpu.VMEM((tm + 2 * HALO, d), BF16)]
                        + [pltpu.VMEM((tm + 2 * HALO, tf // 2), F32)] * 4),
        compiler_params=_cparams("parallel", "arbitrary"),
        name="conv_ffn",
    )(*args)


def _ffn_loop_kernel(*refs, tm, tiles_per_seq, final, n_tiles):
    h_ref, hp_ref, hn_ref, cw_ref, cb_ref, g2_ref = refs[:6]
    pos = 6
    if final:
        fn_ref = refs[pos]
        pos += 1
    x_hbm, wup_hbm, wdn_hbm, o_ref, lhs_ref = refs[pos:pos + 5]
    u_refs = refs[pos + 5:pos + 13]
    wg_buf, wv_buf, wd_buf, x_buf, sem, x_sem = refs[pos + 13:]
    i = pl.program_id(0)
    x_copy = pltpu.make_async_copy(x_hbm.at[pl.ds(pl.multiple_of(i * tm, tm), tm), :], x_buf, x_sem.at[0])
    x_copy.start()
    tf = FFN_TF
    hf = tf // 2
    nj = D_FF // tf
    n_pairs = nj // 2
    assert nj == 2 * n_pairs + 1 and n_pairs % 2 == 1

    def copies(j0, n, slot):
        c0 = j0 * tf if isinstance(j0, int) else pl.multiple_of(j0 * tf, tf)
        w = n * tf
        return (pltpu.make_async_copy(wup_hbm.at[:, pl.ds(c0, w)], wg_buf.at[slot, :, pl.ds(0, w)],
                                      sem.at[slot, 0]),
                pltpu.make_async_copy(wup_hbm.at[:, pl.ds(D_FF + c0, w)], wv_buf.at[slot, :, pl.ds(0, w)],
                                      sem.at[slot, 1]),
                pltpu.make_async_copy(wdn_hbm.at[pl.ds(c0, w), :], wd_buf.at[slot, pl.ds(0, w), :],
                                      sem.at[slot, 2]))

    def start(j0, n, slot):
        for c in copies(j0, n, slot):
            c.start()

    def wait(n, slot):
        for c in copies(0, n, slot):
            c.wait()

    @pl.when(i == 0)
    def _():
        start(0, 2, 0)

    first = (i % tiles_per_seq) == 0
    last = (i % tiles_per_seq) == tiles_per_seq - 1
    lhs_ref[0:HALO, :] = jnp.where(first, jnp.zeros_like(hp_ref[...]), hp_ref[...])
    lhs_ref[HALO:HALO + tm, :] = h_ref[...]
    lhs_ref[HALO + tm:, :] = jnp.where(last, jnp.zeros_like(hn_ref[...]), hn_ref[...])

    def conv(u_ref, col):
        cols = pl.ds(col if isinstance(col, int) else pl.multiple_of(col, hf), hf)
        cw = cw_ref[:, cols]
        return (u_ref[pl.ds(HALO - 1, tm), :] * cw[0:1] + u_ref[pl.ds(HALO, tm), :] * cw[1:2]
                + u_ref[pl.ds(HALO + 1, tm), :] * cw[2:3] + cb_ref[:, cols])

    def step(j, slot, k, us):
        lhs = lhs_ref[...]
        wg, wv, wd = wg_buf.at[slot], wv_buf.at[slot], wd_buf.at[slot]
        c0 = k * tf
        uga, ugb, uva, uvb = us
        uga[...] = _dot(lhs, wg[:, c0:c0 + hf])
        ugb[...] = _dot(lhs, wg[:, c0 + hf:c0 + tf])
        uva[...] = _dot(lhs, wv[:, c0:c0 + hf])
        uvb[...] = _dot(lhs, wv[:, c0 + hf:c0 + tf])
        act_a = (_silu(conv(uga, j * tf)) * conv(uva, D_FF + j * tf)).astype(BF16)
        act_b = (_silu(conv(ugb, j * tf + hf)) * conv(uvb, D_FF + j * tf + hf)).astype(BF16)
        return _dot(act_a, wd[c0:c0 + hf, :]) + _dot(act_b, wd[c0 + hf:c0 + tf, :])

    def pair(p, carry):
        slot = p % 2
        wait(2, slot)

        @pl.when(p < n_pairs - 1)
        def _():
            start(2 * p + 2, 2, 1 - slot)

        @pl.when(p == n_pairs - 1)
        def _():
            start(nj - 1, 1, 1 - slot)

        o_ref[...] += step(2 * p, slot, 0, u_refs[:4]) + step(2 * p + 1, slot, 1, u_refs[4:])
        return carry

    wait(2, 0)
    start(2, 2, 1)
    o_ref[...] = step(0, 0, 0, u_refs[:4]) + step(1, 0, 1, u_refs[4:])
    lax.fori_loop(1, n_pairs, pair, 0)

    wait(1, 1)

    @pl.when(i < n_tiles - 1)
    def _():
        start(0, 2, 0)

    ffn = o_ref[...] + step(nj - 1, 1, 0, u_refs[:4])
    x_copy.wait()
    out = x_buf[...] + g2_ref[0] * ffn
    if final:
        out = _rms(out, fn_ref[...])
    o_ref[...] = out


def _ffn_loop_call(h2, w_up_bf, conv_w, conv_b, w_down_bf, x, ms, final_norm, tm, seq, rows_per_mod):
    m, d = x.shape
    tf = FFN_TF
    assert seq % tm == 0
    tps = seq // tm
    tpm = rows_per_mod // tm
    hb = tm // HALO
    nhb = m // HALO
    n_tiles = m // tm
    final = final_norm is not None
    row = lambda i: (i, 0)
    const = lambda i: (0, 0)
    in_specs = [pl.BlockSpec((tm, d), row),
                pl.BlockSpec((HALO, d), lambda i: (jnp.maximum(i * hb - 1, 0), 0)),
                pl.BlockSpec((HALO, d), lambda i: (jnp.minimum((i + 1) * hb, nhb - 1), 0)),
                pl.BlockSpec((CONV_W, 2 * D_FF), const),
                pl.BlockSpec((1, 2 * D_FF), const),
                _mod_spec(ms, 5, tpm)]
    args = [h2, h2, h2, conv_w, conv_b.reshape(1, 2 * D_FF), ms[0]]
    if final:
        in_specs.append(pl.BlockSpec((1, d), const))
        args.append(final_norm.reshape(1, d))
    in_specs += [pl.BlockSpec(memory_space=pl.ANY)] * 3
    args += [x, w_up_bf, w_down_bf]
    return pl.pallas_call(
        functools.partial(_ffn_loop_kernel, tm=tm, tiles_per_seq=tps, final=final, n_tiles=n_tiles),
        grid=(n_tiles,),
        in_specs=in_specs,
        out_specs=pl.BlockSpec((tm, d), row),
        out_shape=jax.ShapeDtypeStruct((m, d), F32),
        scratch_shapes=([pltpu.VMEM((tm + 2 * HALO, d), BF16)]
                        + [pltpu.VMEM((tm + 2 * HALO, tf // 2), F32)] * 8
                        + [pltpu.VMEM((2, d, 2 * tf), BF16), pltpu.VMEM((2, d, 2 * tf), BF16),
                           pltpu.VMEM((2, 2 * tf, d), BF16), pltpu.VMEM((tm, d), F32),
                           pltpu.SemaphoreType.DMA((2, 3)), pltpu.SemaphoreType.DMA((1,))]),
        compiler_params=pltpu.CompilerParams(dimension_semantics=("arbitrary",),
                                             vmem_limit_bytes=V7X_VMEM_LIMIT_BYTES + 4 * 1024 * 1024),
        name="conv_ffn_loop",
    )(*args)


def _rope_tables(seq, rot_dim):
    f32 = np.float32
    rows = seq // GRID_W
    row = np.repeat(np.arange(rows, dtype=f32), GRID_W)
    col = np.tile(np.arange(GRID_W, dtype=f32), rows)
    axis_dim = rot_dim // 2
    nf = axis_dim // 2
    freqs = f32(ROPE_BASE) ** (-(np.arange(nf, dtype=f32) * f32(2.0) / f32(axis_dim)))
    ar, ac = row[:, None] * freqs, col[:, None] * freqs
    cos = np.concatenate([np.cos(ar)] * 2 + [np.cos(ac)] * 2, axis=-1)
    sin = np.concatenate([np.sin(ar)] * 2 + [np.sin(ac)] * 2, axis=-1)
    first = np.tile(np.concatenate([np.ones((nf,), f32), np.zeros((nf,), f32)]), 2)
    reps = HEAD_DIM // rot_dim
    tables = (np.tile(cos, (1, reps)), np.tile(-sin * first, (1, reps)),
              np.tile(sin * (f32(1.0) - first), (1, reps)))
    return tuple(jnp.asarray(t, dtype=F32) for t in tables)


def kernel(x, c, ctx, c_ctx, w_mod, b_mod, norm_mix, norm_ffn, w_in, da_lambda_q1, da_lambda_k1,
           da_lambda_q2, da_lambda_k2, da_subln, gqa_q_norm, gqa_k_norm, pool_w, pool_scale,
           gqa_out_norm, w_out, w_up, conv_w, conv_b, w_down, final_norm):
    nb, seq, d = x.shape
    n_ctx = ctx.shape[1]
    depth = w_mod.shape[0]
    assert d == D_MODEL and seq % GRID_W == 0 and nb <= 7

    tm_lat = min(512, seq)
    tm_ctx = min(512, nb * n_ctx)
    tq_lat = min(128, seq)
    tq_ctx = min(256, n_ctx)
    tpi_lat = math.gcd(seq // tq_lat, 16)

    c8 = jnp.zeros((8, d), F32).at[:nb].set(c).at[nb].set(c_ctx)
    mods, w_in_bf = _mod_call(c8, w_mod, b_mod, [(w_in, 0)])
    mods_r = mods.reshape(depth * 8 * N_MOD, 1, d)

    rope = _rope_tables(seq, DA_HALF_DIM) + _rope_tables(seq, HEAD_DIM)

    xl = x.reshape(nb * seq, d)
    xc = ctx.reshape(nb * n_ctx, d)

    for l in range(depth):
        last = l == depth - 1
        lambda_init = 0.8 - 0.6 * math.exp(-0.3 * l)
        ms_lat = (mods_r, l, None)
        ms_ctx = (mods_r, l, nb)
        da_params = ((da_lambda_q1[l], da_lambda_k1[l], da_lambda_q2[l], da_lambda_k2[l]),
                     da_subln[l], lambda_init)

        pl_ = _inproj_call(xl, ms_lat, norm_mix[l], w_in_bf, gqa_q_norm[l], gqa_k_norm[l],
                           rope, tm_lat, seq)
        pc_ = _inproj_call(xc, ms_ctx, norm_mix[l], w_in_bf, gqa_q_norm[l], gqa_k_norm[l],
                           None, tm_ctx, nb * n_ctx)
        daq_l, dak_l, dav_l, pool_l, gq_l, gk_l, gv_l = pl_
        daq_c, dak_c, dav_c, pool_c, gq_c, gk_c, gv_c = pc_

        o_da, w_up_bf, w_down_bf = _attn_call(
            daq_l, [dak_c, dak_l], [dav_c, dav_l], [n_ctx, seq], seq, tq_lat, tpi_lat,
            DA_HEADS, 1, 1, BF16, da_params, cast_jobs=[(w_up, l), (w_down, l)])
        o_gqa, w_out_bf, *w_in_next = _attn_call(
            gq_l, [gk_c, gk_l], [gv_c, gv_l], [n_ctx, seq], seq, tq_lat, tpi_lat,
            GQA_Q_HEADS, GQA_GROUP, 1, F32, cast_jobs=[(w_out, l)] + ([] if last else [(w_in, l + 1)]))
        ffn_w = (w_up_bf, conv_w[l], conv_b[l], w_down_bf)
        o_pool = _pool_call(pool_l, pool_w[l], pool_scale[l], seq)
        xl, h2 = _outproj_call(o_da, o_pool, o_gqa, gqa_out_norm[l], w_out_bf, xl, ms_lat,
                               norm_ffn[l], tm_lat, seq)
        xl = _ffn_loop_call(h2, *ffn_w, xl, ms_lat, final_norm if last else None, tm_lat, seq, seq)
        if not last:
            o_da, = _attn_call(daq_c, [dak_c], [dav_c], [n_ctx], n_ctx, tq_ctx, 1,
                               DA_HEADS, 1, DA_HEADS, BF16, da_params)
            o_gqa, = _attn_call(gq_c, [gk_c], [gv_c], [n_ctx], n_ctx, tq_ctx, 1,
                                GQA_Q_HEADS, GQA_GROUP, GQA_KV_HEADS, F32)
            o_pool = _pool_call(pool_c, pool_w[l], pool_scale[l], n_ctx)
            xc, h2 = _outproj_call(o_da, o_pool, o_gqa, gqa_out_norm[l], w_out_bf, xc, ms_ctx,
                                   norm_ffn[l], tm_ctx, nb * n_ctx)
            xc = _ffn_call(h2, *ffn_w, xc, ms_ctx, None, tm_ctx, n_ctx, nb * n_ctx)
            w_in_bf, = w_in_next

    return xl.reshape(nb, seq, d)
```

```python
import functools
import math

import jax
import jax.numpy as jnp
import numpy as np
from jax import lax
from jax.experimental import pallas as pl
from jax.experimental.pallas import tpu as pltpu

F32 = jnp.float32
BF16 = jnp.bfloat16

D_MODEL = 2048
GRID_W = 64
EPS = 1e-6
ROPE_BASE = 10000.0
N_MOD = 6
MOD_ROWS = 8
LOG2E = math.log2(math.e)

DA_HEADS = 6
DA_HALF_DIM = 64
HEAD_DIM = 128
DA_WIDTH = DA_HEADS * HEAD_DIM
POOL_WINDOWS = (2, 4, 8, 16)
POOL_WIDTH = len(POOL_WINDOWS) * HEAD_DIM
GQA_Q_HEADS = 6
GQA_KV_HEADS = 2
GQA_GROUP = GQA_Q_HEADS // GQA_KV_HEADS
GQA_WIDTH = GQA_Q_HEADS * HEAD_DIM
GQA_KV_WIDTH = GQA_KV_HEADS * HEAD_DIM
IN_COLS = 3 * DA_WIDTH + POOL_WIDTH + GQA_WIDTH + 2 * GQA_KV_WIDTH
D_FF = 5632
CONV_W = 3

OFF_DAQ = 0
OFF_DAK = OFF_DAQ + DA_WIDTH
OFF_DAV = OFF_DAK + DA_WIDTH
OFF_POOL = OFF_DAV + DA_WIDTH
OFF_GQ = OFF_POOL + POOL_WIDTH
OFF_GK = OFF_GQ + GQA_WIDTH
OFF_GV = OFF_GK + GQA_KV_WIDTH

V7X_VMEM_LIMIT_BYTES = 56 * 1024 * 1024
HALO = 16
FFN_TF = 512
PROJ_SUB = 256
CAST_STEPS = 16
POOL_PAD = 16


def _cparams(*sem):
    return pltpu.CompilerParams(dimension_semantics=sem, vmem_limit_bytes=V7X_VMEM_LIMIT_BYTES)


def _dot(a, b):
    return jnp.dot(a, b, preferred_element_type=F32)


def _rms(x, gain):
    return x * lax.rsqrt(jnp.mean(x * x, axis=-1, keepdims=True) + EPS) * gain


def _modulate(x, gain, shift, scale):
    return _rms(x, gain) * (1.0 + scale) + shift


def _silu(x):
    return x / (1.0 + jnp.exp2(x * -LOG2E))


def _cast_specs(jobs, step_of, grid_steps):
    in_specs, out_specs, out_shapes, args = [], [], [], []
    n = min(CAST_STEPS, 1 << (grid_steps.bit_length() - 1))
    for w, layer in jobs:
        _, r, c = w.shape
        rb = r // n
        assert rb * n == r and rb % HALO == 0
        blk = lambda *g: jnp.minimum(step_of(*g), n - 1)
        in_specs.append(pl.BlockSpec((None, rb, c), lambda *g, layer=layer, blk=blk: (layer, blk(*g), 0)))
        out_specs.append(pl.BlockSpec((rb, c), lambda *g, blk=blk: (blk(*g), 0)))
        out_shapes.append(jax.ShapeDtypeStruct((r, c), BF16))
        args.append(w)
    return in_specs, out_specs, out_shapes, args


def _run_casts(in_refs, out_refs):
    for src, dst in zip(in_refs, out_refs):
        dst[...] = src[...].astype(BF16)


def _mod_kernel(*refs, n_cast):
    c_ref, w_ref, b_ref = refs[:3]
    cast_in = refs[3:3 + n_cast]
    o_ref = refs[3 + n_cast]
    cast_out = refs[4 + n_cast:]
    s = _silu(c_ref[...]).astype(BF16)
    o_ref[0] = _dot(s, w_ref[0].astype(BF16)) + b_ref[0]
    _run_casts(cast_in, cast_out)


def _mod_call(c8, w_mod, b_mod, cast_jobs, tn=1024):
    depth, d, n = w_mod.shape
    nj = n // tn
    c_in, c_out, c_shapes, c_args = _cast_specs(cast_jobs, lambda l, j: l * nj + j, depth * nj)
    return pl.pallas_call(
        functools.partial(_mod_kernel, n_cast=len(cast_jobs)),
        grid=(depth, nj),
        in_specs=[pl.BlockSpec((MOD_ROWS, d), lambda l, j: (0, 0)),
                  pl.BlockSpec((1, d, tn), lambda l, j: (l, 0, j)),
                  pl.BlockSpec((1, 1, tn), lambda l, j: (l, 0, j))] + c_in,
        out_specs=[pl.BlockSpec((1, MOD_ROWS, tn), lambda l, j: (l, 0, j))] + c_out,
        out_shape=[jax.ShapeDtypeStruct((depth, MOD_ROWS, n), F32)] + c_shapes,
        compiler_params=_cparams("arbitrary", "arbitrary"),
        name="adaln_mod",
    )(c8, w_mod, b_mod.reshape(depth, 1, n), *c_args)


def _rope(x, c, s1, s2, shift):
    return x * c + pltpu.roll(x, HEAD_DIM - shift, 1) * s1 + pltpu.roll(x, shift, 1) * s2


def _inproj_kernel(*refs, use_rope, sub):
    x_ref, sh_ref, sc_ref, nrm_ref, w_ref, qn_ref, kn_ref = refs[:7]
    pos = 7
    rope_refs = ()
    if use_rope:
        rope_refs = refs[pos:pos + 6]
        pos += 6
    daq_ref, dak_ref, dav_ref, pool_ref, gq_ref, gk_ref, gv_ref = refs[pos:]

    def head_cols(i):
        return slice(i * HEAD_DIM, (i + 1) * HEAD_DIM)

    for r in range(x_ref.shape[0] // sub):
        rows = slice(r * sub, (r + 1) * sub)
        h = _modulate(x_ref[rows, :], nrm_ref[...], sh_ref[0], sc_ref[0]).astype(BF16)
        dac, das1, das2, gc, gs1, gs2 = (t[rows, :] for t in rope_refs) if use_rope else (None,) * 6

        def proj(off, width):
            return _dot(h, w_ref[:, off:off + width])

        a = proj(OFF_DAQ, DA_WIDTH)
        for i in range(DA_HEADS):
            t = a[:, head_cols(i)]
            if use_rope:
                t = _rope(t, dac, das1, das2, DA_HALF_DIM // 4)
            daq_ref[rows, head_cols(i)] = (t * (DA_HALF_DIM ** -0.5 * LOG2E)).astype(BF16)
        a = proj(OFF_DAK, DA_WIDTH)
        for i in range(DA_HEADS):
            t = a[:, head_cols(i)]
            if use_rope:
                t = _rope(t, dac, das1, das2, DA_HALF_DIM // 4)
            dak_ref[head_cols(i), rows] = t.T.astype(BF16)
        dav_ref[rows, :] = proj(OFF_DAV, DA_WIDTH).astype(BF16)
        pool_ref[rows, :] = proj(OFF_POOL, POOL_WIDTH)
        a = proj(OFF_GQ, GQA_WIDTH)
        for i in range(GQA_Q_HEADS):
            t = _rms(a[:, head_cols(i)], qn_ref[...])
            if use_rope:
                t = _rope(t, gc, gs1, gs2, HEAD_DIM // 4)
            gq_ref[rows, head_cols(i)] = (t * (HEAD_DIM ** -0.5 * LOG2E)).astype(BF16)
        a = proj(OFF_GK, GQA_KV_WIDTH)
        for i in range(GQA_KV_HEADS):
            t = _rms(a[:, head_cols(i)], kn_ref[...])
            if use_rope:
                t = _rope(t, gc, gs1, gs2, HEAD_DIM // 4)
            gk_ref[head_cols(i), rows] = t.T.astype(BF16)
        gv_ref[rows, :] = proj(OFF_GV, GQA_KV_WIDTH).astype(BF16)


def _mod_spec(ms, chunk, tpm):
    _, layer, fixed_row = ms

    def index(i, *_):
        r = i // tpm if fixed_row is None else fixed_row
        return ((layer * MOD_ROWS + r) * N_MOD + chunk, 0, 0)

    return pl.BlockSpec((1, 1, D_MODEL), index)


def _inproj_call(x, ms, norm_gain, w_in_bf, q_norm, k_norm, rope, tm, rows_per_mod):
    m, d = x.shape
    tpm = rows_per_mod // tm
    row = lambda i: (i, 0)
    const = lambda i: (0, 0)
    shift, scale = ms[0], ms[0]
    in_specs = [pl.BlockSpec((tm, d), row),
                _mod_spec(ms, 0, tpm),
                _mod_spec(ms, 1, tpm),
                pl.BlockSpec((1, d), const),
                pl.BlockSpec((d, IN_COLS), const, pipeline_mode=pl.Buffered(1)),
                pl.BlockSpec((1, HEAD_DIM), const),
                pl.BlockSpec((1, HEAD_DIM), const)]
    args = [x, shift, scale, norm_gain.reshape(1, d), w_in_bf,
            q_norm.reshape(1, HEAD_DIM), k_norm.reshape(1, HEAD_DIM)]
    if rope is not None:
        tps = rope[0].shape[0] // tm
        in_specs += [pl.BlockSpec((tm, HEAD_DIM), lambda i: (i % tps, 0))] * 6
        args += list(rope)
    widths = (DA_WIDTH, DA_WIDTH, DA_WIDTH, POOL_WIDTH, GQA_WIDTH, GQA_KV_WIDTH, GQA_KV_WIDTH)
    dtypes = (BF16, BF16, BF16, F32, BF16, BF16, BF16)
    return pl.pallas_call(
        functools.partial(_inproj_kernel, use_rope=rope is not None, sub=min(tm, PROJ_SUB)),
        grid=(m // tm,),
        in_specs=in_specs,
        out_specs=[pl.BlockSpec((w, tm), lambda i: (0, i)) if k in (1, 5) else pl.BlockSpec((tm, w), row)
                   for k, w in enumerate(widths)],
        out_shape=[jax.ShapeDtypeStruct((w, m) if k in (1, 5) else (m, w), dt)
                   for k, (w, dt) in enumerate(zip(widths, dtypes))],
        compiler_params=_cparams("parallel"),
        name="in_proj",
    )(*args)


def _attn_kernel(*refs, nseg, tq, tiles_per_iter, lambda_init, n_cast):
    da = lambda_init is not None
    q_ref = refs[0]
    k_refs = refs[1:1 + nseg]
    v_refs = refs[1 + nseg:1 + 2 * nseg]
    pos = 1 + 2 * nseg
    if da:
        lq1, lk1, lq2, lk2, subln_ref = refs[pos:pos + 5]
        pos += 5
    cast_in = refs[pos:pos + n_cast]
    o_ref = refs[pos + n_cast]
    cast_out = refs[pos + n_cast + 1:pos + 2 * n_cast + 1]
    kcat, vcat = refs[pos + 2 * n_cast + 1:]
    _run_casts(cast_in, cast_out)

    n_kv = kcat.shape[0]
    n_q = q_ref.shape[1] // HEAD_DIM
    for kv in range(n_kv):
        cols = slice(kv * HEAD_DIM, (kv + 1) * HEAD_DIM)
        off = 0
        for kr, vr in zip(k_refs, v_refs):
            n = vr.shape[0]
            kcat[kv, :, off:off + n] = kr[cols, :]
            vcat[kv, off:off + n, :HEAD_DIM] = vr[:, cols]
            off += n
        vcat[kv, :, HEAD_DIM:] = jnp.ones((off, HEAD_DIM), BF16)

    def attend(q, kv):
        s = _dot(q, kcat[kv])
        e = jnp.exp2(s - jnp.max(s, axis=-1, keepdims=True)).astype(BF16)
        o = _dot(e, vcat[kv])
        return o[:, :HEAD_DIM] / o[:, HEAD_DIM:]

    if da:
        lam = (jnp.exp(jnp.sum(lq1[...] * lk1[...], axis=-1, keepdims=True))
               - jnp.exp(jnp.sum(lq2[...] * lk2[...], axis=-1, keepdims=True)) + lambda_init)
        subln = subln_ref[...]

    def tile(t):
        rows = pl.ds(pl.multiple_of(t * tq, tq), tq)
        for g in range(n_q):
            cols = slice(g * HEAD_DIM, (g + 1) * HEAD_DIM)
            kv = g * n_kv // n_q
            q = q_ref[rows, cols]
            if da:
                lane = lax.broadcasted_iota(jnp.int32, q.shape, 1)
                zero = jnp.zeros_like(q)
                a1 = attend(jnp.where(lane < DA_HALF_DIM, q, zero), kv)
                a2 = attend(jnp.where(lane >= DA_HALF_DIM, q, zero), kv)
                o = _rms(a1 - lam * a2, subln) * (1.0 - lambda_init)
            else:
                o = attend(q, kv)
            o_ref[rows, cols] = o.astype(o_ref.dtype)

    n_iter = q_ref.shape[0] // (tq * tiles_per_iter)

    def body(it, carry):
        for u in range(tiles_per_iter):
            tile(it * tiles_per_iter + u)
        return carry

    lax.fori_loop(0, n_iter, body, 0)


def _attn_call(q, k_segs, v_segs, seg_lens, sq, tq, tiles_per_iter, n_heads, q_per_kv, kv_per_step,
               out_dtype, da_params=None, cast_jobs=()):
    mq = q.shape[0]
    nb = mq // sq
    nseg = len(k_segs)
    sk = sum(seg_lens)
    n_steps = n_heads // (q_per_kv * kv_per_step)
    c_in, c_out, c_shapes, c_args = _cast_specs(cast_jobs, lambda b, h: b * n_steps + h, nb * n_steps)
    qspec = pl.BlockSpec((sq, kv_per_step * q_per_kv * HEAD_DIM), lambda b, h: (b, h))
    kv_specs = [pl.BlockSpec((n, kv_per_step * HEAD_DIM), lambda b, h: (b, h)) for n in seg_lens]
    kt_specs = [pl.BlockSpec((kv_per_step * HEAD_DIM, n), lambda b, h: (h, b)) for n in seg_lens]
    in_specs = [qspec] + kt_specs + kv_specs
    args = [q, *k_segs, *v_segs]
    lambda_init = None
    if da_params is not None:
        lam_vecs, subln, lambda_init = da_params
        vec = lambda n: pl.BlockSpec((1, n), lambda b, h: (0, 0))
        in_specs += [vec(DA_HALF_DIM)] * 4 + [vec(HEAD_DIM)]
        args += [v.reshape(1, DA_HALF_DIM) for v in lam_vecs] + [subln.reshape(1, HEAD_DIM)]
    return pl.pallas_call(
        functools.partial(_attn_kernel, nseg=nseg, tq=tq, tiles_per_iter=tiles_per_iter,
                          lambda_init=lambda_init, n_cast=len(cast_jobs)),
        grid=(nb, n_steps),
        in_specs=in_specs + c_in,
        out_specs=[qspec] + c_out,
        out_shape=[jax.ShapeDtypeStruct((mq, n_heads * HEAD_DIM), out_dtype)] + c_shapes,
        scratch_shapes=[pltpu.VMEM((kv_per_step, HEAD_DIM, sk), BF16),
                        pltpu.VMEM((kv_per_step, sk, 2 * HEAD_DIM), BF16)],
        compiler_params=_cparams("arbitrary", "arbitrary"),
        name="da_attn" if da_params is not None else "gqa_attn",
    )(*args, *c_args)


def _pool_kernel(u_ref, w_ref, ps_ref, o_ref):
    t = u_ref.shape[0]
    row = lax.broadcasted_iota(jnp.int32, (t, HEAD_DIM), 0)
    for g, w in enumerate(POOL_WINDOWS):
        sl = slice(g * HEAD_DIM, (g + 1) * HEAD_DIM)
        u = u_ref[:, sl]
        zeros = jnp.zeros((POOL_PAD, HEAD_DIM), F32)
        ue = jnp.concatenate([zeros, u, zeros], axis=0)
        n = t + 2 * POOL_PAD
        acc = ue + pltpu.roll(ue, 1, 0)
        m = 2
        while m < w:
            acc = pltpu.roll(acc, m // 2, 0) + pltpu.roll(acc, n - m // 2, 0)
            m *= 2
        cnt = jnp.minimum(row - w // 2 + w, t) - jnp.maximum(row - w // 2, 0)
        pooled = acc[POOL_PAD:POOL_PAD + t] / cnt.astype(F32) - u
        mixed = _dot(pooled.astype(BF16), w_ref[g].astype(BF16))
        o_ref[:, sl] = (mixed * ps_ref[:, sl]).astype(BF16)


def _pool_call(u, w_pool, pool_scale, seq):
    m = u.shape[0]
    return pl.pallas_call(
        _pool_kernel,
        grid=(m // seq,),
        in_specs=[pl.BlockSpec((seq, POOL_WIDTH), lambda b: (b, 0)),
                  pl.BlockSpec(w_pool.shape, lambda b: (0, 0, 0)),
                  pl.BlockSpec((1, POOL_WIDTH), lambda b: (0, 0))],
        out_specs=pl.BlockSpec((seq, POOL_WIDTH), lambda b: (b, 0)),
        out_shape=jax.ShapeDtypeStruct((m, POOL_WIDTH), BF16),
        compiler_params=_cparams("parallel"),
        name="pool_mixer",
    )(u, w_pool, pool_scale.reshape(1, POOL_WIDTH))


def _outproj_kernel(oda_ref, opool_ref, ogqa_ref, gn_ref, w_ref, x_ref, g1_ref,
                    nrm2_ref, sh2_ref, sc2_ref, xo_ref, h2_ref, *, sub):
    for r in range(x_ref.shape[0] // sub):
        rows = slice(r * sub, (r + 1) * sub)
        ogn = _rms(ogqa_ref[rows, :], gn_ref[...]).astype(BF16)
        cat = jnp.concatenate([oda_ref[rows, :], opool_ref[rows, :], ogn], axis=1)
        xn = x_ref[rows, :] + g1_ref[0] * _dot(cat, w_ref[...])
        xo_ref[rows, :] = xn
        h2_ref[rows, :] = _modulate(xn, nrm2_ref[...], sh2_ref[0], sc2_ref[0]).astype(BF16)


def _outproj_call(o_da, o_pool, o_gqa, gqa_out_norm, w_out_bf, x, ms, norm_ffn, tm, rows_per_mod):
    m, d = x.shape
    tpm = rows_per_mod // tm
    row = lambda i: (i, 0)
    const = lambda i: (0, 0)
    g1 = sh2 = sc2 = ms[0]
    return pl.pallas_call(
        functools.partial(_outproj_kernel, sub=min(tm, PROJ_SUB)),
        grid=(m // tm,),
        in_specs=[pl.BlockSpec((tm, DA_WIDTH), row),
                  pl.BlockSpec((tm, POOL_WIDTH), row),
                  pl.BlockSpec((tm, GQA_WIDTH), row),
                  pl.BlockSpec((1, GQA_WIDTH), const),
                  pl.BlockSpec((d, d), const, pipeline_mode=pl.Buffered(1)),
                  pl.BlockSpec((tm, d), row),
                  _mod_spec(ms, 2, tpm),
                  pl.BlockSpec((1, d), const),
                  _mod_spec(ms, 3, tpm), _mod_spec(ms, 4, tpm)],
        out_specs=[pl.BlockSpec((tm, d), row), pl.BlockSpec((tm, d), row)],
        out_shape=[jax.ShapeDtypeStruct((m, d), F32), jax.ShapeDtypeStruct((m, d), BF16)],
        compiler_params=_cparams("parallel"),
        name="out_proj",
    )(o_da, o_pool, o_gqa, gqa_out_norm.reshape(1, GQA_WIDTH), w_out_bf, x, g1,
      norm_ffn.reshape(1, d), sh2, sc2)


def _ffn_kernel(*refs, tm, tiles_per_seq, seq_starts, final, n_tiles):
    h_ref, hp_ref, hn_ref, cw_ref, cb_ref, g2_ref = refs[:6]
    pos = 6
    if final:
        fn_ref = refs[pos]
        pos += 1
    x_hbm, wup_hbm, wdn_hbm, o_ref, lhs_ref = refs[pos:pos + 5]
    u_refs = refs[pos + 5:pos + 13]
    wg_buf, wv_buf, wd_buf, x_buf, sem, x_sem = refs[pos + 13:]
    i = pl.program_id(0)
    x_copy = pltpu.make_async_copy(x_hbm.at[pl.ds(pl.multiple_of(i * tm, tm), tm), :], x_buf, x_sem.at[0])
    x_copy.start()
    tf = FFN_TF
    hf = tf // 2
    nj = D_FF // tf
    n_pairs = nj // 2
    assert nj == 2 * n_pairs + 1 and n_pairs % 2 == 1

    def copies(j0, n, slot):
        c0 = j0 * tf if isinstance(j0, int) else pl.multiple_of(j0 * tf, tf)
        w = n * tf
        return (pltpu.make_async_copy(wup_hbm.at[:, pl.ds(c0, w)], wg_buf.at[slot, :, pl.ds(0, w)],
                                      sem.at[slot, 0]),
                pltpu.make_async_copy(wup_hbm.at[:, pl.ds(D_FF + c0, w)], wv_buf.at[slot, :, pl.ds(0, w)],
                                      sem.at[slot, 1]),
                pltpu.make_async_copy(wdn_hbm.at[pl.ds(c0, w), :], wd_buf.at[slot, pl.ds(0, w), :],
                                      sem.at[slot, 2]))

    def start(j0, n, slot):
        for c in copies(j0, n, slot):
            c.start()

    def wait(n, slot):
        for c in copies(0, n, slot):
            c.wait()

    @pl.when(i == 0)
    def _():
        start(0, 2, 0)

    first = (i % tiles_per_seq) == 0
    last = (i % tiles_per_seq) == tiles_per_seq - 1
    lhs_ref[0:HALO, :] = jnp.where(first, jnp.zeros_like(hp_ref[...]), hp_ref[...])
    lhs_ref[HALO:HALO + tm, :] = h_ref[...]
    lhs_ref[HALO + tm:, :] = jnp.where(last, jnp.zeros_like(hn_ref[...]), hn_ref[...])

    def conv(u_ref, col):
        cols = pl.ds(col if isinstance(col, int) else pl.multiple_of(col, hf), hf)
        cw, cb = cw_ref[:, cols], cb_ref[:, cols]

        def taps(start, n, fix=None):
            prev, nxt = u_ref[pl.ds(HALO + start - 1, n), :], u_ref[pl.ds(HALO + start + 1, n), :]
            if fix is not None:
                row = lax.broadcasted_iota(jnp.int32, prev.shape, 0) + start
                prev = jnp.where(row == fix, 0.0, prev)
                nxt = jnp.where(row == fix - 1, 0.0, nxt)
            return prev * cw[0:1] + u_ref[pl.ds(HALO + start, n), :] * cw[1:2] + nxt * cw[2:3] + cb

        c = taps(0, tm)
        for b in seq_starts:
            c = jnp.concatenate([c[:b - 8], taps(b - 8, 16, fix=b), c[b + 8:]], axis=0)
        return c

    def step(j, slot, k, us):
        lhs = lhs_ref[...]
        wg, wv, wd = wg_buf.at[slot], wv_buf.at[slot], wd_buf.at[slot]
        c0 = k * tf
        uga, ugb, uva, uvb = us
        uga[...] = _dot(lhs, wg[:, c0:c0 + hf])
        ugb[...] = _dot(lhs, wg[:, c0 + hf:c0 + tf])
        uva[...] = _dot(lhs, wv[:, c0:c0 + hf])
        uvb[...] = _dot(lhs, wv[:, c0 + hf:c0 + tf])
        act_a = (_silu(conv(uga, j * tf)) * conv(uva, D_FF + j * tf)).astype(BF16)
        act_b = (_silu(conv(ugb, j * tf + hf)) * conv(uvb, D_FF + j * tf + hf)).astype(BF16)
        return _dot(act_a, wd[c0:c0 + hf, :]) + _dot(act_b, wd[c0 + hf:c0 + tf, :])

    def pair(p, carry):
        slot = p % 2
        wait(2, slot)

        @pl.when(p < n_pairs - 1)
        def _():
            start(2 * p + 2, 2, 1 - slot)

        @pl.when(p == n_pairs - 1)
        def _():
            start(nj - 1, 1, 1 - slot)

        o_ref[...] += step(2 * p, slot, 0, u_refs[:4]) + step(2 * p + 1, slot, 1, u_refs[4:])
        return carry

    wait(2, 0)
    start(2, 2, 1)
    o_ref[...] = step(0, 0, 0, u_refs[:4]) + step(1, 0, 1, u_refs[4:])
    lax.fori_loop(1, n_pairs, pair, 0)

    wait(1, 1)

    @pl.when(i < n_tiles - 1)
    def _():
        start(0, 2, 0)

    ffn = o_ref[...] + step(nj - 1, 1, 0, u_refs[:4])
    x_copy.wait()
    out = x_buf[...] + g2_ref[0] * ffn
    if final:
        out = _rms(out, fn_ref[...])
    o_ref[...] = out


def _ffn_call(h2, w_up_bf, conv_w, conv_b, w_down_bf, x, ms, final_norm, tm, seq, rows_per_mod):
    m, d = x.shape
    tf = FFN_TF
    assert seq % tm == 0 or tm % seq == 0
    tps = max(seq // tm, 1)
    seq_starts = tuple(range(seq, tm, seq))
    tpm = rows_per_mod // tm
    hb = tm // HALO
    nhb = m // HALO
    n_tiles = m // tm
    final = final_norm is not None
    row = lambda i: (i, 0)
    const = lambda i: (0, 0)
    in_specs = [pl.BlockSpec((tm, d), row),
                pl.BlockSpec((HALO, d), lambda i: (jnp.maximum(i * hb - 1, 0), 0)),
                pl.BlockSpec((HALO, d), lambda i: (jnp.minimum((i + 1) * hb, nhb - 1), 0)),
                pl.BlockSpec((CONV_W, 2 * D_FF), const),
                pl.BlockSpec((1, 2 * D_FF), const),
                _mod_spec(ms, 5, tpm)]
    args = [h2, h2, h2, conv_w, conv_b.reshape(1, 2 * D_FF), ms[0]]
    if final:
        in_specs.append(pl.BlockSpec((1, d), const))
        args.append(final_norm.reshape(1, d))
    in_specs += [pl.BlockSpec(memory_space=pl.ANY)] * 3
    args += [x, w_up_bf, w_down_bf]
    return pl.pallas_call(
        functools.partial(_ffn_kernel, tm=tm, tiles_per_seq=tps, seq_starts=seq_starts, final=final,
                          n_tiles=n_tiles),
        grid=(n_tiles,),
        in_specs=in_specs,
        out_specs=pl.BlockSpec((tm, d), row),
        out_shape=jax.ShapeDtypeStruct((m, d), F32),
        scratch_shapes=([pltpu.VMEM((tm + 2 * HALO, d), BF16)]
                        + [pltpu.VMEM((tm + 2 * HALO, tf // 2), F32)] * 8
                        + [pltpu.VMEM((2, d, 2 * tf), BF16), pltpu.VMEM((2, d, 2 * tf), BF16),
                           pltpu.VMEM((2, 2 * tf, d), BF16), pltpu.VMEM((tm, d), F32),
                           pltpu.SemaphoreType.DMA((2, 3)), pltpu.SemaphoreType.DMA((1,))]),
        compiler_params=pltpu.CompilerParams(dimension_semantics=("arbitrary",),
                                             vmem_limit_bytes=V7X_VMEM_LIMIT_BYTES + 4 * 1024 * 1024),
        name="conv_ffn",
    )(*args)


def _rope_tables(seq, rot_dim):
    f32 = np.float32
    rows = seq // GRID_W
    row = np.repeat(np.arange(rows, dtype=f32), GRID_W)
    col = np.tile(np.arange(GRID_W, dtype=f32), rows)
    axis_dim = rot_dim // 2
    nf = axis_dim // 2
    freqs = f32(ROPE_BASE) ** (-(np.arange(nf, dtype=f32) * f32(2.0) / f32(axis_dim)))
    ar, ac = row[:, None] * freqs, col[:, None] * freqs
    cos = np.concatenate([np.cos(ar)] * 2 + [np.cos(ac)] * 2, axis=-1)
    sin = np.concatenate([np.sin(ar)] * 2 + [np.sin(ac)] * 2, axis=-1)
    first = np.tile(np.concatenate([np.ones((nf,), f32), np.zeros((nf,), f32)]), 2)
    reps = HEAD_DIM // rot_dim
    tables = (np.tile(cos, (1, reps)), np.tile(-sin * first, (1, reps)),
              np.tile(sin * (f32(1.0) - first), (1, reps)))
    return tuple(jnp.asarray(t, dtype=F32) for t in tables)


def kernel(x, c, ctx, c_ctx, w_mod, b_mod, norm_mix, norm_ffn, w_in, da_lambda_q1, da_lambda_k1,
           da_lambda_q2, da_lambda_k2, da_subln, gqa_q_norm, gqa_k_norm, pool_w, pool_scale,
           gqa_out_norm, w_out, w_up, conv_w, conv_b, w_down, final_norm):
    nb, seq, d = x.shape
    n_ctx = ctx.shape[1]
    depth = w_mod.shape[0]
    assert d == D_MODEL and seq % GRID_W == 0 and nb < MOD_ROWS

    tm_lat = min(512, seq)
    tm_ctx = min(512, nb * n_ctx)
    tq_lat = min(128, seq)
    tq_ctx = min(256, n_ctx)
    tpi_lat = math.gcd(seq // tq_lat, 16)

    c8 = jnp.zeros((MOD_ROWS, d), F32).at[:nb].set(c).at[nb].set(c_ctx)
    mods, w_in_bf = _mod_call(c8, w_mod, b_mod, [(w_in, 0)])
    mods_r = mods.reshape(depth * MOD_ROWS * N_MOD, 1, d)

    rope = _rope_tables(seq, DA_HALF_DIM) + _rope_tables(seq, HEAD_DIM)

    xl = x.reshape(nb * seq, d)
    xc = ctx.reshape(nb * n_ctx, d)

    for l in range(depth):
        last = l == depth - 1
        lambda_init = 0.8 - 0.6 * math.exp(-0.3 * l)
        ms_lat = (mods_r, l, None)
        ms_ctx = (mods_r, l, nb)
        da_params = ((da_lambda_q1[l], da_lambda_k1[l], da_lambda_q2[l], da_lambda_k2[l]),
                     da_subln[l], lambda_init)

        pl_ = _inproj_call(xl, ms_lat, norm_mix[l], w_in_bf, gqa_q_norm[l], gqa_k_norm[l],
                           rope, tm_lat, seq)
        pc_ = _inproj_call(xc, ms_ctx, norm_mix[l], w_in_bf, gqa_q_norm[l], gqa_k_norm[l],
                           None, tm_ctx, nb * n_ctx)
        daq_l, dak_l, dav_l, pool_l, gq_l, gk_l, gv_l = pl_
        daq_c, dak_c, dav_c, pool_c, gq_c, gk_c, gv_c = pc_

        o_da, w_up_bf, w_down_bf = _attn_call(
            daq_l, [dak_c, dak_l], [dav_c, dav_l], [n_ctx, seq], seq, tq_lat, tpi_lat,
            DA_HEADS, 1, 1, BF16, da_params, cast_jobs=[(w_up, l), (w_down, l)])
        o_gqa, w_out_bf, *w_in_next = _attn_call(
            gq_l, [gk_c, gk_l], [gv_c, gv_l], [n_ctx, seq], seq, tq_lat, tpi_lat,
            GQA_Q_HEADS, GQA_GROUP, 1, F32, cast_jobs=[(w_out, l)] + ([] if last else [(w_in, l + 1)]))
        ffn_w = (w_up_bf, conv_w[l], conv_b[l], w_down_bf)
        o_pool = _pool_call(pool_l, pool_w[l], pool_scale[l], seq)
        xl, h2 = _outproj_call(o_da, o_pool, o_gqa, gqa_out_norm[l], w_out_bf, xl, ms_lat,
                               norm_ffn[l], tm_lat, seq)
        xl = _ffn_call(h2, *ffn_w, xl, ms_lat, final_norm if last else None, tm_lat, seq, seq)
        if not last:
            o_da, = _attn_call(daq_c, [dak_c], [dav_c], [n_ctx], n_ctx, tq_ctx, 1,
                               DA_HEADS, 1, DA_HEADS, BF16, da_params)
            o_gqa, = _attn_call(gq_c, [gk_c], [gv_c], [n_ctx], n_ctx, tq_ctx, 1,
                                GQA_Q_HEADS, GQA_GROUP, GQA_KV_HEADS, F32)
            o_pool = _pool_call(pool_c, pool_w[l], pool_scale[l], n_ctx)
            xc, h2 = _outproj_call(o_da, o_pool, o_gqa, gqa_out_norm[l], w_out_bf, xc, ms_ctx,
                                   norm_ffn[l], tm_ctx, nb * n_ctx)
            xc = _ffn_call(h2, *ffn_w, xc, ms_ctx, None, tm_ctx, n_ctx, nb * n_ctx)
            w_in_bf, = w_in_next

    return xl.reshape(nb, seq, d)
```
